```python
import math
import jax, jax.numpy as jnp
from jax import lax
import numpy as np

D_MODEL = 1024
BATCH = 32
SEQ = 256
DEPTH = 1
DEC_BATCH = 8
DEC_SEQ = 2048
PAST_LEN = 256

GRID_W = 64
Q_BLOCK = 128
ROPE_THETA = 10000.0
LN_EPS = 1e-5
A_HEADS = 8
A_HEAD_DIM = 64
A_V_DIM = 2 * A_HEAD_DIM
B_HEADS = 16
B_KV_HEADS = 4
B_HEAD_DIM = 64
A_QK_W = A_HEADS * 2 * A_HEAD_DIM
A_V_W = A_HEADS * A_V_DIM
B_Q_W = B_HEADS * B_HEAD_DIM
B_KV_W = B_KV_HEADS * B_HEAD_DIM
IN_SPLITS = (A_QK_W, A_QK_W, A_V_W, B_Q_W, B_KV_W, B_KV_W, D_MODEL, D_MODEL)
IN_W = sum(IN_SPLITS)
IN_OFFSETS = [sum(IN_SPLITS[:i + 1]) for i in range(len(IN_SPLITS) - 1)]
N_GROUPS = 4
EXPERTS_PER_GROUP = 4
N_EXPERTS = N_GROUPS * EXPERTS_PER_GROUP
TOP_K_FINE = 2
EXPERT_FF = 512
DEEPNORM_ALPHA = (2 * DEPTH) ** 0.25
DEEPNORM_BETA = (8 * DEPTH) ** -0.25

kernel_name = "hybrid_diffattn_gqa_hiermoe_prefix_ctx_step"


def ln_plain(x):
    xf = x.astype(jnp.float32)
    mu = jnp.mean(xf, -1, keepdims=True)
    var = jnp.mean(jnp.square(xf - mu), -1, keepdims=True)
    return ((xf - mu) * lax.rsqrt(var + LN_EPS)).astype(x.dtype)


def ln_affine(x, g, b):
    xf = x.astype(jnp.float32)
    mu = jnp.mean(xf, -1, keepdims=True)
    var = jnp.mean(jnp.square(xf - mu), -1, keepdims=True)
    y = (xf - mu) * lax.rsqrt(var + LN_EPS) * g.astype(jnp.float32) + b.astype(jnp.float32)
    return y.astype(x.dtype)


def rms_norm(x, g):
    xf = x.astype(jnp.float32)
    y = xf * lax.rsqrt(jnp.mean(jnp.square(xf), -1, keepdims=True) + LN_EPS)
    return (y * g.astype(jnp.float32)).astype(x.dtype)


def axial_rope_tables(n_tokens, head_dim):
    n_rows = n_tokens // GRID_W
    rows = jnp.repeat(jnp.arange(n_rows, dtype=jnp.float32), GRID_W)
    cols = jnp.tile(jnp.arange(GRID_W, dtype=jnp.float32), n_rows)
    axis_dim = head_dim // 2
    freqs = ROPE_THETA ** (-jnp.arange(0, axis_dim, 2, dtype=jnp.float32) / axis_dim)
    ang = jnp.concatenate([rows[:, None] * freqs, cols[:, None] * freqs], -1)
    return jnp.cos(ang), jnp.sin(ang)


def apply_rope(x, cos, sin):
    xf = x.astype(jnp.float32).reshape(*x.shape[:-1], -1, 2)
    x1, x2 = xf[..., 0], xf[..., 1]
    c = cos[None, :, None, :]
    s = sin[None, :, None, :]
    out = jnp.stack([x1 * c - x2 * s, x1 * s + x2 * c], -1).reshape(x.shape)
    return out.astype(x.dtype)


def sweep_query_blocks(fn, q):
    b, s = q.shape[:2]
    nb = s // Q_BLOCK
    qb = jnp.moveaxis(q.reshape(b, nb, Q_BLOCK, *q.shape[2:]), 1, 0)
    out = lax.map(fn, qb)
    return jnp.moveaxis(out, 0, 1).reshape(b, s, out.shape[-1])


def diff_attention(q, k, v, lam, lambda_init, subln_g):
    scale = A_HEAD_DIM ** -0.5

    def block(qb):
        s = jnp.einsum('bqhnd,bkhnd->bhnqk', qb, k).astype(jnp.float32) * scale
        p = jax.nn.softmax(s, axis=-1)
        a = p[:, :, 0] - lam * p[:, :, 1]
        o = jnp.einsum('bhqk,bkhe->bqhe', a.astype(v.dtype), v)
        o = rms_norm(o, subln_g) * (1.0 - lambda_init)
        return o.reshape(o.shape[0], o.shape[1], -1)

    return sweep_query_blocks(block, q)


def gqa_attention(q, k, v):
    scale = B_HEAD_DIM ** -0.5
    n_rep = B_HEADS // B_KV_HEADS

    def block(qb):
        b, nq = qb.shape[:2]
        qg = qb.reshape(b, nq, B_KV_HEADS, n_rep, B_HEAD_DIM)
        s = jnp.einsum('bqhgd,bkhd->bhgqk', qg, k).astype(jnp.float32) * scale
        p = jax.nn.softmax(s, axis=-1)
        o = jnp.einsum('bhgqk,bkhd->bqhgd', p.astype(v.dtype), v)
        return o.reshape(b, nq, -1)

    return sweep_query_blocks(block, q)


def hier_moe(h, w_coarse, b_coarse, w_fine, b_fine, w_gate_up, w_down):
    b, s, d = h.shape
    t = h.reshape(-1, d)
    p_coarse = jax.nn.softmax((t @ w_coarse + b_coarse).astype(jnp.float32), axis=-1)
    p_grp, grp = lax.top_k(p_coarse, 1)
    grp_oh = jax.nn.one_hot(grp[:, 0], N_GROUPS, dtype=jnp.float32)
    fine_logits = (t @ w_fine + b_fine).astype(jnp.float32).reshape(-1, N_GROUPS, EXPERTS_PER_GROUP)
    fine_sel = jnp.einsum('tg,tge->te', grp_oh, fine_logits)
    p_fine = jax.nn.softmax(fine_sel, axis=-1)
    w_top, e_top = lax.top_k(p_fine, TOP_K_FINE)
    w_top = w_top / jnp.sum(w_top, -1, keepdims=True)
    within = jnp.einsum('tk,tke->te', w_top, jax.nn.one_hot(e_top, EXPERTS_PER_GROUP, dtype=jnp.float32))
    combine = ((p_grp * grp_oh)[:, :, None] * within[:, None, :]).reshape(-1, N_EXPERTS).astype(h.dtype)
    out = jnp.zeros_like(t)
    for e in range(N_EXPERTS):
        gg, uu = jnp.split(t @ w_gate_up[e], 2, axis=-1)
        out = out + combine[:, e:e + 1] * ((jax.nn.silu(gg) * uu) @ w_down[e])
    return out.reshape(b, s, d)


def trunk_layer(l, x, cond, ctx, rope_a, rope_b, p):
    b, s, _ = x.shape
    mod = jax.nn.silu(cond) @ p['w_ada'][l] + p['b_ada'][l]
    sh1, sc1, g1, sh2, sc2, g2 = jnp.split(mod[:, None, :], 6, axis=-1)

    h = ln_plain(x) * (1 + sc1) + sh1
    aq, ak, av, bq, bk, bv, ga, gb = jnp.split(h @ p['w_in'][l], IN_OFFSETS, axis=-1)
    aq = aq.reshape(b, s, A_HEADS, 2, A_HEAD_DIM)
    ak = ak.reshape(b, s, A_HEADS, 2, A_HEAD_DIM)
    av = av.reshape(b, s, A_HEADS, A_V_DIM)
    bq = rms_norm(bq.reshape(b, s, B_HEADS, B_HEAD_DIM), p['b_q_g'][l])
    bk = rms_norm(bk.reshape(b, s, B_KV_HEADS, B_HEAD_DIM), p['b_k_g'][l])
    bv = bv.reshape(b, s, B_KV_HEADS, B_HEAD_DIM)
    ctx_out = (ak, av, bk, bv)

    if rope_a is not None:
        aq = apply_rope(aq.reshape(b, s, 2 * A_HEADS, A_HEAD_DIM), *rope_a).reshape(aq.shape)
        ak = apply_rope(ak.reshape(b, s, 2 * A_HEADS, A_HEAD_DIM), *rope_a).reshape(ak.shape)
        bq = apply_rope(bq, *rope_b)
        bk = apply_rope(bk, *rope_b)

    if ctx is None:
        kA, vA, kB, vB = ak, av, bk, bv
    else:
        kA = jnp.concatenate([ak, ctx[0]], axis=1)
        vA = jnp.concatenate([av, ctx[1]], axis=1)
        kB = jnp.concatenate([bk, ctx[2]], axis=1)
        vB = jnp.concatenate([bv, ctx[3]], axis=1)

    lambda_init = 0.8 - 0.6 * math.exp(-0.3 * l)
    lam = (jnp.exp(jnp.sum(p['lambda_q1'][l].astype(jnp.float32) * p['lambda_k1'][l].astype(jnp.float32)))
           - jnp.exp(jnp.sum(p['lambda_q2'][l].astype(jnp.float32) * p['lambda_k2'][l].astype(jnp.float32)))
           + lambda_init)
    o_a = diff_attention(aq, kA, vA, lam, lambda_init, p['a_subln_g'][l])
    o_b = gqa_attention(bq, kB, vB)
    merged = jax.nn.sigmoid(ga) * (o_a @ p['w_br_a'][l]) + jax.nn.sigmoid(gb) * (o_b @ p['w_br_b'][l])
    x = ln_affine(DEEPNORM_ALPHA * x + g1 * (merged @ p['w_o'][l]), p['ln1_g'][l], p['ln1_b'][l])

    h = ln_plain(x) * (1 + sc2) + sh2
    y = hier_moe(h, p['w_coarse'][l], p['b_coarse'][l], p['w_fine'][l], p['b_fine'][l],
                 p['w_gate_up'][l], p['w_down'][l])
    x = ln_affine(DEEPNORM_ALPHA * x + g2 * y, p['ln2_g'][l], p['ln2_b'][l])
    return x, ctx_out


def setup_inputs(seed: int = 0) -> dict:
    key = jax.random.key(seed)
    ks = iter(jax.random.split(key, 40))
    f32 = jnp.float32

    def nrm(shape, scale=1.0):
        return jax.random.normal(next(ks), shape, f32) * scale

    D = D_MODEL
    return {
        'x_prompt': nrm((BATCH, SEQ, D)),
        'x_sample': nrm((DEC_BATCH, DEC_SEQ, D)),
        'cache_a_k': nrm((DEC_BATCH, DEPTH, PAST_LEN, A_HEADS, 2, A_HEAD_DIM)),
        'cache_a_v': nrm((DEC_BATCH, DEPTH, PAST_LEN, A_HEADS, A_V_DIM)),
        'cache_b_k': nrm((DEC_BATCH, DEPTH, PAST_LEN, B_KV_HEADS, B_HEAD_DIM)),
        'cache_b_v': nrm((DEC_BATCH, DEPTH, PAST_LEN, B_KV_HEADS, B_HEAD_DIM)),
        'c': nrm((DEC_BATCH, D)),
        'c_ctx': nrm((D,)),
        'w_ada': nrm((DEPTH, D, 6 * D), 0.5 * D ** -0.5),
        'b_ada': nrm((DEPTH, 6 * D), 0.02),
        'w_in': nrm((DEPTH, D, IN_W), D ** -0.5),
        'lambda_q1': nrm((DEPTH, A_HEAD_DIM), 0.1),
        'lambda_k1': nrm((DEPTH, A_HEAD_DIM), 0.1),
        'lambda_q2': nrm((DEPTH, A_HEAD_DIM), 0.1),
        'lambda_k2': nrm((DEPTH, A_HEAD_DIM), 0.1),
        'a_subln_g': 1.0 + nrm((DEPTH, A_V_DIM), 0.02),
        'b_q_g': 1.0 + nrm((DEPTH, B_HEAD_DIM), 0.02),
        'b_k_g': 1.0 + nrm((DEPTH, B_HEAD_DIM), 0.02),
        'w_br_a': nrm((DEPTH, A_V_W, D), A_V_W ** -0.5),
        'w_br_b': nrm((DEPTH, B_Q_W, D), B_Q_W ** -0.5),
        'w_o': nrm((DEPTH, D, D), DEEPNORM_BETA * D ** -0.5),
        'ln1_g': 1.0 + nrm((DEPTH, D), 0.02),
        'ln1_b': nrm((DEPTH, D), 0.02),
        'w_coarse': nrm((DEPTH, D, N_GROUPS), D ** -0.5),
        'b_coarse': nrm((DEPTH, N_GROUPS), 0.01),
        'w_fine': nrm((DEPTH, D, N_EXPERTS), D ** -0.5),
        'b_fine': nrm((DEPTH, N_EXPERTS), 0.01),
        'w_gate_up': nrm((DEPTH, N_EXPERTS, D, 2 * EXPERT_FF), D ** -0.5),
        'w_down': nrm((DEPTH, N_EXPERTS, EXPERT_FF, D), DEEPNORM_BETA * EXPERT_FF ** -0.5),
        'ln2_g': 1.0 + nrm((DEPTH, D), 0.02),
        'ln2_b': nrm((DEPTH, D), 0.02),
    }


def reference(x_prompt, x_sample, cache_a_k, cache_a_v, cache_b_k, cache_b_v, c, c_ctx,
              w_ada, b_ada, w_in, lambda_q1, lambda_k1, lambda_q2, lambda_k2, a_subln_g,
              b_q_g, b_k_g, w_br_a, w_br_b, w_o, ln1_g, ln1_b, w_coarse, b_coarse,
              w_fine, b_fine, w_gate_up, w_down, ln2_g, ln2_b):
    params = dict(w_ada=w_ada, b_ada=b_ada, w_in=w_in, lambda_q1=lambda_q1, lambda_k1=lambda_k1,
                  lambda_q2=lambda_q2, lambda_k2=lambda_k2, a_subln_g=a_subln_g, b_q_g=b_q_g,
                  b_k_g=b_k_g, w_br_a=w_br_a, w_br_b=w_br_b, w_o=w_o, ln1_g=ln1_g, ln1_b=ln1_b,
                  w_coarse=w_coarse, b_coarse=b_coarse, w_fine=w_fine, b_fine=b_fine,
                  w_gate_up=w_gate_up, w_down=w_down, ln2_g=ln2_g, ln2_b=ln2_b)

    cond_ctx = c_ctx[None, :]
    xp = x_prompt
    a_k_list, a_v_list, b_k_list, b_v_list = [], [], [], []
    for l in range(DEPTH):
        xp, (ak, av, bk, bv) = trunk_layer(l, xp, cond_ctx, None, None, None, params)
        a_k_list.append(ak)
        a_v_list.append(av)
        b_k_list.append(bk)
        b_v_list.append(bv)
    y_prompt = xp
    new_a_k = jnp.stack(a_k_list, axis=1)
    new_a_v = jnp.stack(a_v_list, axis=1)
    new_b_k = jnp.stack(b_k_list, axis=1)
    new_b_v = jnp.stack(b_v_list, axis=1)

    n_lat = x_sample.shape[1]
    rope_a = axial_rope_tables(n_lat, A_HEAD_DIM)
    rope_b = axial_rope_tables(n_lat, B_HEAD_DIM)
    xs = x_sample
    for l in range(DEPTH):
        ctx = (cache_a_k[:, l], cache_a_v[:, l], cache_b_k[:, l], cache_b_v[:, l])
        xs, _ = trunk_layer(l, xs, c, ctx, rope_a, rope_b, params)
    y_sample = xs

    return (y_prompt, y_sample, new_a_k, new_a_v, new_b_k, new_b_v)
```

```python
import functools
import math

import jax
import jax.numpy as jnp
from jax import lax
from jax.experimental import pallas as pl
from jax.experimental.pallas import tpu as pltpu

F32 = jnp.float32
BF16 = jnp.bfloat16

D_MODEL = 1024
GRID_W = 64
ROPE_THETA = 10000.0
LN_EPS = 1e-5
HEAD_DIM = 64
A_HEADS = 8
B_HEADS = 16
B_KV_HEADS = 4
N_GROUPS = 4
EXPERTS_PER_GROUP = 4
N_EXPERTS = 16
EXPERT_FF = 512
DEPTH = 1
DEEPNORM_ALPHA = (2 * DEPTH) ** 0.25
LAMBDA_INIT = 0.8 - 0.6 * math.exp(-0.3 * 0)
QKV_W = 4608
GATE_W = 2048
LANES = 128
MXU_DIM = 256
Q_SCALE = HEAD_DIM ** -0.5 * math.log2(math.e)
VMEM_LIMIT = 56 * 1024 * 1024

TOKEN_TILE = 256
Q_TILE = 128
KEY_CHUNK = 256
MOE_TILE = 1024


def _cparams(sem):
    return pltpu.CompilerParams(dimension_semantics=sem, vmem_limit_bytes=VMEM_LIMIT)


def _ln_plain(x):
    mu = jnp.mean(x, axis=-1, keepdims=True)
    xc = x - mu
    var = jnp.mean(xc * xc, axis=-1, keepdims=True)
    return xc * lax.rsqrt(var + LN_EPS)


def _mod_kernel(c_ref, w_ref, b_ref, o_ref):
    c = c_ref[...]
    s = (c * jax.nn.sigmoid(c)).astype(BF16)
    o_ref[...] = jnp.dot(s, w_ref[...].astype(BF16), preferred_element_type=F32) + b_ref[...]


def _modulation(cond, w_ada, b_ada):
    n, d = cond.shape
    tn = 1536
    return pl.pallas_call(
        _mod_kernel,
        out_shape=jax.ShapeDtypeStruct((n, 6 * d), F32),
        grid=(6 * d // tn,),
        in_specs=[pl.BlockSpec((n, d), lambda j: (0, 0)),
                  pl.BlockSpec((d, tn), lambda j: (0, j)),
                  pl.BlockSpec((1, tn), lambda j: (0, j))],
        out_specs=pl.BlockSpec((n, tn), lambda j: (0, j)),
        compiler_params=_cparams(("arbitrary",)),
        name="adaln_mod",
    )(cond, w_ada, b_ada)


def _rope(x, c, se, so):
    outs = []
    for j in range(x.shape[1] // LANES):
        xc = x[:, j * LANES:(j + 1) * LANES]
        outs.append(xc * c + pltpu.roll(xc, LANES - 1, 1) * se + pltpu.roll(xc, 1, 1) * so)
    return outs[0] if len(outs) == 1 else jnp.concatenate(outs, axis=1)


def _head_rms(x, bd):
    outs = []
    for j in range(x.shape[1] // MXU_DIM):
        xc = x[:, j * MXU_DIM:(j + 1) * MXU_DIM]
        ms = jnp.dot((xc * xc).astype(BF16), bd, preferred_element_type=F32)
        outs.append(xc * lax.rsqrt(ms + LN_EPS))
    return outs[0] if len(outs) == 1 else jnp.concatenate(outs, axis=1)


def _dup_heads(x):
    outs = []
    lane = lax.broadcasted_iota(jnp.int32, (x.shape[0], LANES), 1)
    lo = lane < HEAD_DIM
    for j in range(x.shape[1] // LANES):
        xc = x[:, j * LANES:(j + 1) * LANES]
        r = pltpu.roll(xc, HEAD_DIM, 1)
        outs.append(jnp.where(lo, xc, r))
        outs.append(jnp.where(lo, r, xc))
    return jnp.concatenate(outs, axis=1)


def _inproj_kernel(*refs, use_rope, emit_ctx):
    it = iter(refs)
    x_ref, mod_ref, w_ref, gq_ref, gk_ref, bd_ref = (next(it) for _ in range(6))
    if use_rope:
        c_ref, se_ref, so_ref = (next(it) for _ in range(3))
    qa_ref, ka_ref, vat_ref, qb_ref, kbd_ref, vbt_ref = (next(it) for _ in range(6))
    if emit_ctx:
        nak_ref, nav_ref, nbk_ref, nbv_ref = (next(it) for _ in range(4))

    x = x_ref[...]
    mod = mod_ref[0]
    h = (_ln_plain(x) * (1.0 + mod[1:2, :]) + mod[0:1, :]).astype(BF16)
    bd = bd_ref[...]
    if use_rope:
        c, se, so = c_ref[...], se_ref[...], so_ref[...]

    def proj(lo, hi):
        return jnp.dot(h, w_ref[:, lo:hi], preferred_element_type=F32)

    aq = proj(0, 1024)
    if use_rope:
        aq = _rope(aq, c, se, so)
    qa_ref[...] = (aq * Q_SCALE).astype(BF16)

    ak = proj(1024, 2048)
    if emit_ctx:
        nak_ref[...] = ak
    if use_rope:
        ak = _rope(ak, c, se, so)
    ka_ref[...] = ak.astype(BF16)

    av = proj(2048, 3072)
    if emit_ctx:
        nav_ref[...] = av
    vat_ref[0] = av.T.astype(BF16)

    bq = _head_rms(proj(3072, 4096), bd) * gq_ref[...]
    if use_rope:
        bq = _rope(bq, c, se, so)
    qb_ref[...] = (bq * Q_SCALE).astype(BF16)

    bk = _head_rms(proj(4096, 4352), bd) * gk_ref[...]
    if emit_ctx:
        nbk_ref[...] = bk
    if use_rope:
        bk = _rope(bk, c, se, so)
    kbd_ref[...] = _dup_heads(bk).astype(BF16)

    bv = proj(4352, 4608)
    if emit_ctx:
        nbv_ref[...] = bv
    vbt_ref[0] = bv.T.astype(BF16)


def _inproj(x2d, mod, w_qkv, gq, gk, bd, rope, n_batch, seq, emit_ctx):
    t = x2d.shape[0]
    tm = TOKEN_TILE
    tiles_per_batch = seq // tm
    tiles_per_mod = t // mod.shape[0] // tm
    use_rope = rope is not None
    row = lambda i: (i, 0)
    const = lambda i: (0, 0)
    in_specs = [pl.BlockSpec((tm, D_MODEL), row),
                pl.BlockSpec((1, 6, D_MODEL), lambda i: (i // tiles_per_mod, 0, 0)),
                pl.BlockSpec((D_MODEL, QKV_W), const),
                pl.BlockSpec((1, 1024), const),
                pl.BlockSpec((1, 256), const),
                pl.BlockSpec((MXU_DIM, MXU_DIM), const)]
    args = [x2d, mod, w_qkv, gq, gk, bd]
    if use_rope:
        tab = pl.BlockSpec((tm, LANES), lambda i: (i % tiles_per_batch, 0))
        in_specs += [tab, tab, tab]
        args += list(rope)
    vt_idx = lambda i: (i // tiles_per_batch, 0, i % tiles_per_batch)
    out_shape = [jax.ShapeDtypeStruct((t, 1024), BF16),
                 jax.ShapeDtypeStruct((t, 1024), BF16),
                 jax.ShapeDtypeStruct((n_batch, 1024, seq), BF16),
                 jax.ShapeDtypeStruct((t, 1024), BF16),
                 jax.ShapeDtypeStruct((t, 512), BF16),
                 jax.ShapeDtypeStruct((n_batch, 256, seq), BF16)]
    out_specs = [pl.BlockSpec((tm, 1024), row),
                 pl.BlockSpec((tm, 1024), row),
                 pl.BlockSpec((1, 1024, tm), vt_idx),
                 pl.BlockSpec((tm, 1024), row),
                 pl.BlockSpec((tm, 512), row),
                 pl.BlockSpec((1, 256, tm), vt_idx)]
    if emit_ctx:
        out_shape += [jax.ShapeDtypeStruct((t, 1024), F32),
                      jax.ShapeDtypeStruct((t, 1024), F32),
                      jax.ShapeDtypeStruct((t, 256), F32),
                      jax.ShapeDtypeStruct((t, 256), F32)]
        out_specs += [pl.BlockSpec((tm, 1024), row),
                      pl.BlockSpec((tm, 1024), row),
                      pl.BlockSpec((tm, 256), row),
                      pl.BlockSpec((tm, 256), row)]
    return pl.pallas_call(
        functools.partial(_inproj_kernel, use_rope=use_rope, emit_ctx=emit_ctx),
        out_shape=out_shape,
        grid=(t // tm,),
        in_specs=in_specs,
        out_specs=out_specs,
        compiler_params=_cparams(("arbitrary",)),
        name="inproj_ctx" if emit_ctx else "inproj_lat",
    )(*args)


def _attn_kernel(*refs, diff, has_cache, n_keys, q_chunk):
    it = iter(refs)
    q_ref, k_ref, vt_ref = (next(it) for _ in range(3))
    if has_cache:
        kc_ref, vtc_ref = (next(it) for _ in range(2))
    if diff:
        lam_ref, g_ref = (next(it) for _ in range(2))
    o_ref, s_ref, p_ref = (next(it) for _ in range(3))

    segs = [(k_ref, n_keys)]
    if has_cache:
        segs.append((kc_ref, kc_ref.shape[0]))
    total = sum(n for _, n in segs)

    if diff:
        lv = lam_ref[...]
        lam = (jnp.exp(jnp.sum(lv[0:1] * lv[1:2], axis=1, keepdims=True))
               - jnp.exp(jnp.sum(lv[2:3] * lv[3:4], axis=1, keepdims=True)) + LAMBDA_INIT)
        gcol = g_ref[...] * (1.0 - LAMBDA_INIT)

    def body(j, carry):
        q = q_ref[pl.ds(pl.multiple_of(j * Q_TILE, Q_TILE), Q_TILE), :]
        lane = lax.broadcasted_iota(jnp.int32, q.shape, 1)
        zero = jnp.zeros_like(q)
        qs = jnp.concatenate([jnp.where(lane < HEAD_DIM, q, zero),
                              jnp.where(lane >= HEAD_DIM, q, zero)], axis=0)
        m = None
        off = 0
        for kr, n in segs:
            for c in range(n // KEY_CHUNK):
                kc = kr[c * KEY_CHUNK:(c + 1) * KEY_CHUNK, :]
                s = lax.dot_general(kc, qs, (((1,), (1,)), ((), ())), preferred_element_type=F32)
                s_ref[off:off + KEY_CHUNK, :] = s
                mc = jnp.max(s, axis=0, keepdims=True)
                m = mc if m is None else jnp.maximum(m, mc)
                off += KEY_CHUNK
        l = jnp.zeros((1, 2 * Q_TILE), F32)
        for c in range(total // KEY_CHUNK):
            p = jnp.exp2(s_ref[c * KEY_CHUNK:(c + 1) * KEY_CHUNK, :] - m)
            l = l + jnp.sum(p, axis=0, keepdims=True)
            p_ref[c * KEY_CHUNK:(c + 1) * KEY_CHUNK, :] = p.astype(BF16)
        o12 = jnp.dot(vt_ref[0], p_ref[0:n_keys, :], preferred_element_type=F32)
        if has_cache:
            o12 = o12 + jnp.dot(vtc_ref[0], p_ref[n_keys:total, :], preferred_element_type=F32)
        on = o12 * (1.0 / l)
        if diff:
            o = on[:, :Q_TILE] - lam * on[:, Q_TILE:]
            o = o * lax.rsqrt(jnp.mean(o * o, axis=0, keepdims=True) + LN_EPS) * gcol
        else:
            o = jnp.concatenate([on[:, :Q_TILE], on[:, Q_TILE:]], axis=0)
        o_ref[pl.ds(pl.multiple_of(j * Q_TILE, Q_TILE), Q_TILE), :] = o.T.astype(o_ref.dtype)
        return carry

    lax.fori_loop(0, q_chunk // Q_TILE, body, 0)


def _attention(q, k, vt, cache, diff_params, n_batch, seq, q_chunk, name):
    diff = diff_params is not None
    has_cache = cache is not None
    t = q.shape[0]
    nqc = seq // q_chunk
    dv = 128 if diff else 64
    kv_unit = (lambda u: u) if diff else (lambda u: u // 2)
    in_specs = [pl.BlockSpec((q_chunk, LANES), lambda b, u, i: (b * nqc + i, u)),
                pl.BlockSpec((seq, LANES), lambda b, u, i: (b, kv_unit(u))),
                pl.BlockSpec((1, dv, seq), lambda b, u, i: (b, kv_unit(u), 0))]
    args = [q, k, vt]
    total = seq
    if has_cache:
        kc, vtc = cache
        past = vtc.shape[2]
        total += past
        in_specs += [pl.BlockSpec((past, LANES), lambda b, u, i: (b, kv_unit(u))),
                     pl.BlockSpec((1, dv, past), lambda b, u, i: (b, kv_unit(u), 0))]
        args += [kc, vtc]
    if diff:
        lam_vecs, gcol = diff_params
        in_specs += [pl.BlockSpec((4, HEAD_DIM), lambda b, u, i: (0, 0)),
                     pl.BlockSpec((LANES, 1), lambda b, u, i: (0, 0))]
        args += [lam_vecs, gcol]
    return pl.pallas_call(
        functools.partial(_attn_kernel, diff=diff, has_cache=has_cache, n_keys=seq, q_chunk=q_chunk),
        out_shape=jax.ShapeDtypeStruct((t, 1024), BF16),
        grid=(n_batch, 8, nqc),
        in_specs=in_specs,
        out_specs=pl.BlockSpec((q_chunk, LANES), lambda b, u, i: (b * nqc + i, u)),
        scratch_shapes=[pltpu.VMEM((total, 2 * Q_TILE), F32),
                        pltpu.VMEM((total, 2 * Q_TILE), BF16)],
        compiler_params=_cparams(("arbitrary", "arbitrary", "arbitrary")),
        name=name,
    )(*args)


def _split_bf16(x):
    hi = x.astype(BF16)
    lo = (x - hi.astype(F32)).astype(BF16)
    return hi, lo


def _post_kernel(x_ref, oa_ref, ob_ref, mod_ref, wg_ref, wa_ref, wb_ref, wo_ref, ln1g_ref, ln1b_ref,
                 wr_ref, br_ref, x1_ref, h2_ref, comb_ref):
    x = x_ref[...]
    mod = mod_ref[0]
    h = (_ln_plain(x) * (1.0 + mod[1:2, :]) + mod[0:1, :]).astype(BF16)
    gates = jnp.dot(h, wg_ref[...], preferred_element_type=F32)
    ma = jnp.dot(oa_ref[...], wa_ref[...], preferred_element_type=F32)
    mb = jnp.dot(ob_ref[...], wb_ref[...], preferred_element_type=F32)
    merged = jax.nn.sigmoid(gates[:, :1024]) * ma + jax.nn.sigmoid(gates[:, 1024:]) * mb
    mo = jnp.dot(merged.astype(BF16), wo_ref[...], preferred_element_type=F32)
    x1 = _ln_plain(DEEPNORM_ALPHA * x + mod[2:3, :] * mo) * ln1g_ref[...] + ln1b_ref[...]
    x1_ref[...] = x1
    h2 = _ln_plain(x1) * (1.0 + mod[4:5, :]) + mod[3:4, :]
    h2_ref[...] = h2.astype(BF16)

    nt = (((1,), (1,)), ((), ()))
    w_hi, w_lo = _split_bf16(wr_ref[...])
    h_hi, h_lo = _split_bf16(h2)
    lt = (lax.dot_general(w_hi, h_hi, nt, preferred_element_type=F32)
          + lax.dot_general(w_hi, h_lo, nt, preferred_element_type=F32)
          + lax.dot_general(w_lo, h_hi, nt, preferred_element_type=F32)) + br_ref[...]

    def top1(rows):
        best, idx = rows[0], jnp.zeros_like(rows[0], jnp.int32)
        for k in range(1, len(rows)):
            better = rows[k] > best
            best = jnp.where(better, rows[k], best)
            idx = jnp.where(better, k, idx)
        return best, idx

    coarse = [lt[g:g + 1, :] for g in range(N_GROUPS)]
    cmax, grp = top1(coarse)
    p_grp = 1.0 / sum(jnp.exp(cg - cmax) for cg in coarse)
    fine = []
    for e in range(EXPERTS_PER_GROUP):
        fe = lt[N_GROUPS + e:N_GROUPS + e + 1, :]
        for g in range(1, N_GROUPS):
            r = N_GROUPS + g * EXPERTS_PER_GROUP + e
            fe = jnp.where(grp == g, lt[r:r + 1, :], fe)
        fine.append(fe)
    fmax, e1 = top1(fine)
    ex = [jnp.exp(f - fmax) for f in fine]
    den = sum(ex)
    pf = [v / den for v in ex]
    neg = jnp.full_like(fmax, -jnp.inf)
    _, e2 = top1([jnp.where(e1 == e, neg, pf[e]) for e in range(EXPERTS_PER_GROUP)])
    p1 = sum(jnp.where(e1 == e, pf[e], 0.0) for e in range(EXPERTS_PER_GROUP))
    p2 = sum(jnp.where(e2 == e, pf[e], 0.0) for e in range(EXPERTS_PER_GROUP))
    wsum = p1 + p2
    rows = []
    for g in range(N_GROUPS):
        for e in range(EXPERTS_PER_GROUP):
            within = jnp.where(e1 == e, p1 / wsum, 0.0) + jnp.where(e2 == e, p2 / wsum, 0.0)
            rows.append(jnp.where(grp == g, p_grp * within, 0.0))
    comb_t = jnp.concatenate(rows + [jnp.zeros((LANES - N_EXPERTS, x.shape[0]), F32)], axis=0)
    comb_ref[...] = comb_t.T


def _post(x2d, oa, ob, mod, w_gate, w_a, w_b, w_o, ln1g, ln1b, wr_t, br_col, name):
    t = x2d.shape[0]
    tm = TOKEN_TILE
    tiles_per_batch = t // mod.shape[0] // tm
    row = lambda i: (i, 0)
    const = lambda i: (0, 0)
    return pl.pallas_call(
        _post_kernel,
        out_shape=[jax.ShapeDtypeStruct((t, D_MODEL), F32),
                   jax.ShapeDtypeStruct((t, D_MODEL), BF16),
                   jax.ShapeDtypeStruct((t, LANES), F32)],
        grid=(t // tm,),
        in_specs=[pl.BlockSpec((tm, D_MODEL), row),
                  pl.BlockSpec((tm, 1024), row),
                  pl.BlockSpec((tm, 1024), row),
                  pl.BlockSpec((1, 6, D_MODEL), lambda i: (i // tiles_per_batch, 0, 0)),
                  pl.BlockSpec((D_MODEL, GATE_W), const),
                  pl.BlockSpec((1024, D_MODEL), const),
                  pl.BlockSpec((1024, D_MODEL), const),
                  pl.BlockSpec((D_MODEL, D_MODEL), const),
                  pl.BlockSpec((1, D_MODEL), const),
                  pl.BlockSpec((1, D_MODEL), const),
                  pl.BlockSpec((LANES, D_MODEL), const),
                  pl.BlockSpec((LANES, 1), const)],
        out_specs=[pl.BlockSpec((tm, D_MODEL), row),
                   pl.BlockSpec((tm, D_MODEL), row),
                   pl.BlockSpec((tm, LANES), row)],
        compiler_params=_cparams(("arbitrary",)),
        name=name,
    )(x2d, oa, ob, mod, w_gate, w_a, w_b, w_o, ln1g, ln1b, wr_t, br_col)


def _moe_kernel(h_ref, comb_ref, wgu_ref, wd_ref, x1_ref, mod_ref, g_ref, b_ref, o_ref, acc_ref):
    e = pl.program_id(1)

    @pl.when(e == 0)
    def _():
        acc_ref[...] = jnp.zeros_like(acc_ref)

    gu = jnp.dot(h_ref[...], wgu_ref[0], preferred_element_type=F32)
    gg, uu = gu[:, :EXPERT_FF], gu[:, EXPERT_FF:]
    act = (gg * jax.nn.sigmoid(gg) * uu).astype(BF16)
    y = jnp.dot(act, wd_ref[0], preferred_element_type=F32)
    comb = comb_ref[...]
    lane = lax.broadcasted_iota(jnp.int32, comb.shape, 1)
    c = jnp.sum(jnp.where(lane == e, comb, 0.0), axis=1, keepdims=True)
    acc_ref[...] += c * y

    @pl.when(e == N_EXPERTS - 1)
    def _():
        mod = mod_ref[0]
        z = DEEPNORM_ALPHA * x1_ref[...] + mod[5:6, :] * acc_ref[...]
        o_ref[...] = _ln_plain(z) * g_ref[...] + b_ref[...]


def _moe(h2, comb, w_gu, w_d, x1, mod, ln2g, ln2b, name):
    t = h2.shape[0]
    tm = MOE_TILE
    tiles_per_batch = t // mod.shape[0] // tm
    row = lambda i, e: (i, 0)
    const = lambda i, e: (0, 0)
    return pl.pallas_call(
        _moe_kernel,
        out_shape=jax.ShapeDtypeStruct((t, D_MODEL), F32),
        grid=(t // tm, N_EXPERTS),
        in_specs=[pl.BlockSpec((tm, D_MODEL), row),
                  pl.BlockSpec((tm, LANES), row),
                  pl.BlockSpec((1, D_MODEL, 2 * EXPERT_FF), lambda i, e: (e, 0, 0)),
                  pl.BlockSpec((1, EXPERT_FF, D_MODEL), lambda i, e: (e, 0, 0)),
                  pl.BlockSpec((tm, D_MODEL), row),
                  pl.BlockSpec((1, 6, D_MODEL), lambda i, e: (i // tiles_per_batch, 0, 0)),
                  pl.BlockSpec((1, D_MODEL), const),
                  pl.BlockSpec((1, D_MODEL), const)],
        out_specs=pl.BlockSpec((tm, D_MODEL), row),
        scratch_shapes=[pltpu.VMEM((tm, D_MODEL), F32)],
        compiler_params=_cparams(("arbitrary", "arbitrary")),
        name=name,
    )(h2, comb, w_gu, w_d, x1, mod, ln2g, ln2b)


def _rope_tables(n_tokens):
    n_rows = n_tokens // GRID_W
    rows = jnp.repeat(jnp.arange(n_rows, dtype=F32), GRID_W)
    cols = jnp.tile(jnp.arange(GRID_W, dtype=F32), n_rows)
    axis_dim = HEAD_DIM // 2
    freqs = ROPE_THETA ** (-jnp.arange(0, axis_dim, 2, dtype=F32) / axis_dim)
    ang = jnp.concatenate([rows[:, None] * freqs, cols[:, None] * freqs], -1)
    ang = jnp.tile(jnp.repeat(ang, 2, axis=1), (1, LANES // HEAD_DIM))
    even = (jnp.arange(LANES) % 2 == 0)[None, :]
    cos, sin = jnp.cos(ang), jnp.sin(ang)
    return cos, jnp.where(even, -sin, 0.0), jnp.where(even, 0.0, sin)


def kernel(x_prompt, x_sample, cache_a_k, cache_a_v, cache_b_k, cache_b_v, c, c_ctx, w_ada, b_ada, w_in,
           lambda_q1, lambda_k1, lambda_q2, lambda_k2, a_subln_g, b_q_g, b_k_g, w_br_a, w_br_b, w_o,
           ln1_g, ln1_b, w_coarse, b_coarse, w_fine, b_fine, w_gate_up, w_down, ln2_g, ln2_b):
    n_ctx, s_ctx, d = x_prompt.shape
    n_lat, s_lat, _ = x_sample.shape
    past = cache_a_k.shape[2]

    w_qkv = w_in[0, :, :QKV_W].astype(BF16)
    w_gate = w_in[0, :, QKV_W:].astype(BF16)
    w_a, w_b, w_out = w_br_a[0].astype(BF16), w_br_b[0].astype(BF16), w_o[0].astype(BF16)
    w_gu, w_d = w_gate_up[0].astype(BF16), w_down[0].astype(BF16)
    gq = jnp.tile(b_q_g[0], B_HEADS)[None, :]
    gk = jnp.tile(b_k_g[0], B_KV_HEADS)[None, :]
    blk = jnp.arange(MXU_DIM) // HEAD_DIM
    bd = jnp.where(blk[:, None] == blk[None, :], 1.0 / HEAD_DIM, 0.0).astype(BF16)
    lam_vecs = jnp.concatenate([lambda_q1, lambda_k1, lambda_q2, lambda_k2], axis=0)
    gcol = a_subln_g[0][:, None]
    wr_t = jnp.zeros((LANES, d), F32).at[:N_GROUPS].set(w_coarse[0].T).at[N_GROUPS:N_GROUPS + N_EXPERTS].set(w_fine[0].T)
    br_col = jnp.zeros((LANES, 1), F32).at[:N_GROUPS, 0].set(b_coarse[0]).at[N_GROUPS:N_GROUPS + N_EXPERTS, 0].set(b_fine[0])
    ln1g, ln1b, ln2g, ln2b = ln1_g, ln1_b, ln2_g, ln2_b

    cond = jnp.concatenate([c_ctx[None, :], c], axis=0)
    mod = _modulation(cond, w_ada[0], b_ada).reshape(1 + n_lat, 6, d)
    mod_ctx, mod_lat = mod[:1], mod[1:]

    cak = cache_a_k[:, 0].reshape(n_lat * past, 1024).astype(BF16)
    cavt = jnp.swapaxes(cache_a_v[:, 0].reshape(n_lat, past, 1024), 1, 2).astype(BF16)
    cbk = jnp.repeat(cache_b_k[:, 0], 2, axis=2).reshape(n_lat * past, 512).astype(BF16)
    cbvt = jnp.swapaxes(cache_b_v[:, 0].reshape(n_lat, past, 256), 1, 2).astype(BF16)

    diff_params = (lam_vecs, gcol)

    def tail(x2d, oa, ob, mod_x, tag):
        x1, h2, comb = _post(x2d, oa, ob, mod_x, w_gate, w_a, w_b, w_out, ln1g, ln1b, wr_t, br_col, "post_" + tag)
        return _moe(h2, comb, w_gu, w_d, x1, mod_x, ln2g, ln2b, "moe_" + tag)

    xp = x_prompt.reshape(n_ctx * s_ctx, d)
    qa, ka, vat, qb, kbd, vbt, nak, nav, nbk, nbv = _inproj(xp, mod_ctx, w_qkv, gq, gk, bd, None, n_ctx, s_ctx, True)
    oa = _attention(qa, ka, vat, None, diff_params, n_ctx, s_ctx, s_ctx, "attn_a_ctx")
    ob = _attention(qb, kbd, vbt, None, None, n_ctx, s_ctx, s_ctx, "attn_b_ctx")
    y_prompt = tail(xp, oa, ob, mod_ctx, "ctx").reshape(n_ctx, s_ctx, d)

    xs = x_sample.reshape(n_lat * s_lat, d)
    qa, ka, vat, qb, kbd, vbt = _inproj(xs, mod_lat, w_qkv, gq, gk, bd, _rope_tables(s_lat), n_lat, s_lat, False)
    oa = _attention(qa, ka, vat, (cak, cavt), diff_params, n_lat, s_lat, 1024, "attn_a_lat")
    ob = _attention(qb, kbd, vbt, (cbk, cbvt), None, n_lat, s_lat, 1024, "attn_b_lat")
    y_sample = tail(xs, oa, ob, mod_lat, "lat").reshape(n_lat, s_lat, d)

    return (y_prompt, y_sample,
            nak.reshape(n_ctx, 1, s_ctx, A_HEADS, 2, HEAD_DIM),
            nav.reshape(n_ctx, 1, s_ctx, A_HEADS, 2 * HEAD_DIM),
            nbk.reshape(n_ctx, 1, s_ctx, B_KV_HEADS, HEAD_DIM),
            nbv.reshape(n_ctx, 1, s_ctx, B_KV_HEADS, HEAD_DIM))
```

```python
import functools
import math

import jax
import jax.numpy as jnp
from jax import lax
from jax.experimental import pallas as pl
from jax.experimental.pallas import tpu as pltpu

F32 = jnp.float32
BF16 = jnp.bfloat16

D_MODEL = 1024
GRID_W = 64
ROPE_THETA = 10000.0
LN_EPS = 1e-5
HEAD_DIM = 64
A_HEADS = 8
B_HEADS = 16
B_KV_HEADS = 4
N_GROUPS = 4
EXPERTS_PER_GROUP = 4
N_EXPERTS = 16
EXPERT_FF = 512
DEPTH = 1
DEEPNORM_ALPHA = (2 * DEPTH) ** 0.25
LAMBDA_INIT = 0.8 - 0.6 * math.exp(-0.3 * 0)
QKV_W = 4608
GATE_W = 2048
LANES = 128
MXU_DIM = 256
Q_SCALE = HEAD_DIM ** -0.5 * math.log2(math.e)
VMEM_LIMIT = 56 * 1024 * 1024

TOKEN_TILE = 256
Q_TILE = 128
KEY_CHUNK = 256
MOE_TILE = 1024


def _cparams(sem):
    return pltpu.CompilerParams(dimension_semantics=sem, vmem_limit_bytes=VMEM_LIMIT)


def _ln_plain(x):
    mu = jnp.mean(x, axis=-1, keepdims=True)
    xc = x - mu
    var = jnp.mean(xc * xc, axis=-1, keepdims=True)
    return xc * lax.rsqrt(var + LN_EPS)


def _mod_kernel(c_ref, w_ref, b_ref, o_ref):
    c = c_ref[...]
    s = (c * jax.nn.sigmoid(c)).astype(BF16)
    o_ref[...] = jnp.dot(s, w_ref[...].astype(BF16), preferred_element_type=F32) + b_ref[...]


def _modulation(cond, w_ada, b_ada):
    n, d = cond.shape
    tn = 1536
    return pl.pallas_call(
        _mod_kernel,
        out_shape=jax.ShapeDtypeStruct((n, 6 * d), F32),
        grid=(6 * d // tn,),
        in_specs=[pl.BlockSpec((n, d), lambda j: (0, 0)),
                  pl.BlockSpec((d, tn), lambda j: (0, j)),
                  pl.BlockSpec((1, tn), lambda j: (0, j))],
        out_specs=pl.BlockSpec((n, tn), lambda j: (0, j)),
        compiler_params=_cparams(("arbitrary",)),
        name="adaln_mod",
    )(cond, w_ada, b_ada)


def _rope(x, c, se, so):
    outs = []
    for j in range(x.shape[1] // LANES):
        xc = x[:, j * LANES:(j + 1) * LANES]
        outs.append(xc * c + pltpu.roll(xc, LANES - 1, 1) * se + pltpu.roll(xc, 1, 1) * so)
    return outs[0] if len(outs) == 1 else jnp.concatenate(outs, axis=1)


def _head_rms(x, bd):
    outs = []
    for j in range(x.shape[1] // MXU_DIM):
        xc = x[:, j * MXU_DIM:(j + 1) * MXU_DIM]
        ms = jnp.dot((xc * xc).astype(BF16), bd, preferred_element_type=F32)
        outs.append(xc * lax.rsqrt(ms + LN_EPS))
    return outs[0] if len(outs) == 1 else jnp.concatenate(outs, axis=1)


def _dup_heads(x):
    outs = []
    lane = lax.broadcasted_iota(jnp.int32, (x.shape[0], LANES), 1)
    lo = lane < HEAD_DIM
    for j in range(x.shape[1] // LANES):
        xc = x[:, j * LANES:(j + 1) * LANES]
        r = pltpu.roll(xc, HEAD_DIM, 1)
        outs.append(jnp.where(lo, xc, r))
        outs.append(jnp.where(lo, r, xc))
    return jnp.concatenate(outs, axis=1)


def _inproj_kernel(*refs, use_rope, emit_ctx):
    it = iter(refs)
    x_ref, mod_ref, w_ref, gq_ref, gk_ref, bd_ref = (next(it) for _ in range(6))
    if use_rope:
        c_ref, se_ref, so_ref = (next(it) for _ in range(3))
    qa_ref, ka_ref, vat_ref, qb_ref, kbd_ref, vbt_ref = (next(it) for _ in range(6))
    if emit_ctx:
        nak_ref, nav_ref, nbk_ref, nbv_ref = (next(it) for _ in range(4))

    x = x_ref[...]
    mod = mod_ref[0]
    h = (_ln_plain(x) * (1.0 + mod[1:2, :]) + mod[0:1, :]).astype(BF16)
    bd = bd_ref[...]
    if use_rope:
        c, se, so = c_ref[...], se_ref[...], so_ref[...]

    def proj(lo, hi):
        return jnp.dot(h, w_ref[:, lo:hi], preferred_element_type=F32)

    aq = proj(0, 1024)
    if use_rope:
        aq = _rope(aq, c, se, so)
    qa_ref[...] = (aq * Q_SCALE).astype(BF16)

    ak = proj(1024, 2048)
    if emit_ctx:
        nak_ref[...] = ak
    if use_rope:
        ak = _rope(ak, c, se, so)
    ka_ref[...] = ak.astype(BF16)

    av = proj(2048, 3072)
    if emit_ctx:
        nav_ref[...] = av
    vat_ref[0] = av.T.astype(BF16)

    bq = _head_rms(proj(3072, 4096), bd) * gq_ref[...]
    if use_rope:
        bq = _rope(bq, c, se, so)
    qb_ref[...] = (bq * Q_SCALE).astype(BF16)

    bk = _head_rms(proj(4096, 4352), bd) * gk_ref[...]
    if emit_ctx:
        nbk_ref[...] = bk
    if use_rope:
        bk = _rope(bk, c, se, so)
    kbd_ref[...] = _dup_heads(bk).astype(BF16)

    bv = proj(4352, 4608)
    if emit_ctx:
        nbv_ref[...] = bv
    vbt_ref[0] = bv.T.astype(BF16)


def _inproj(x2d, mod, w_qkv, gq, gk, bd, rope, n_batch, seq, emit_ctx):
    t = x2d.shape[0]
    tm = TOKEN_TILE
    tiles_per_batch = seq // tm
    tiles_per_mod = t // mod.shape[0] // tm
    use_rope = rope is not None
    row = lambda i: (i, 0)
    const = lambda i: (0, 0)
    in_specs = [pl.BlockSpec((tm, D_MODEL), row),
                pl.BlockSpec((1, 6, D_MODEL), lambda i: (i // tiles_per_mod, 0, 0)),
                pl.BlockSpec((D_MODEL, QKV_W), const),
                pl.BlockSpec((1, 1024), const),
                pl.BlockSpec((1, 256), const),
                pl.BlockSpec((MXU_DIM, MXU_DIM), const)]
    args = [x2d, mod, w_qkv, gq, gk, bd]
    if use_rope:
        tab = pl.BlockSpec((tm, LANES), lambda i: (i % tiles_per_batch, 0))
        in_specs += [tab, tab, tab]
        args += list(rope)
    vt_idx = lambda i: (i // tiles_per_batch, 0, i % tiles_per_batch)
    out_shape = [jax.ShapeDtypeStruct((t, 1024), BF16),
                 jax.ShapeDtypeStruct((t, 1024), BF16),
                 jax.ShapeDtypeStruct((n_batch, 1024, seq), BF16),
                 jax.ShapeDtypeStruct((t, 1024), BF16),
                 jax.ShapeDtypeStruct((t, 512), BF16),
                 jax.ShapeDtypeStruct((n_batch, 256, seq), BF16)]
    out_specs = [pl.BlockSpec((tm, 1024), row),
                 pl.BlockSpec((tm, 1024), row),
                 pl.BlockSpec((1, 1024, tm), vt_idx),
                 pl.BlockSpec((tm, 1024), row),
                 pl.BlockSpec((tm, 512), row),
                 pl.BlockSpec((1, 256, tm), vt_idx)]
    if emit_ctx:
        out_shape += [jax.ShapeDtypeStruct((t, 1024), F32),
                      jax.ShapeDtypeStruct((t, 1024), F32),
                      jax.ShapeDtypeStruct((t, 256), F32),
                      jax.ShapeDtypeStruct((t, 256), F32)]
        out_specs += [pl.BlockSpec((tm, 1024), row),
                      pl.BlockSpec((tm, 1024), row),
                      pl.BlockSpec((tm, 256), row),
                      pl.BlockSpec((tm, 256), row)]
    return pl.pallas_call(
        functools.partial(_inproj_kernel, use_rope=use_rope, emit_ctx=emit_ctx),
        out_shape=out_shape,
        grid=(t // tm,),
        in_specs=in_specs,
        out_specs=out_specs,
        compiler_params=_cparams(("arbitrary",)),
        name="inproj_ctx" if emit_ctx else "inproj_lat",
    )(*args)


def _attn_kernel(*refs, diff, has_cache, n_keys, q_chunk):
    it = iter(refs)
    q_ref, k_ref, vt_ref = (next(it) for _ in range(3))
    if has_cache:
        kc_ref, vtc_ref = (next(it) for _ in range(2))
    if diff:
        lam_ref, g_ref = (next(it) for _ in range(2))
    o_ref, s0_ref, s1_ref = (next(it) for _ in range(3))
    s_bufs = (s0_ref, s1_ref)

    segs = [(k_ref, n_keys)]
    vsegs = [(vt_ref, n_keys)]
    if has_cache:
        segs.append((kc_ref, kc_ref.shape[0]))
        vsegs.append((vtc_ref, kc_ref.shape[0]))

    if diff:
        lv = lam_ref[...]
        lam = (jnp.exp(jnp.sum(lv[0:1] * lv[1:2], axis=1, keepdims=True))
               - jnp.exp(jnp.sum(lv[2:3] * lv[3:4], axis=1, keepdims=True)) + LAMBDA_INIT)
        gcol = g_ref[...] * (1.0 - LAMBDA_INIT)

    n_sub = q_chunk // Q_TILE

    def score_stage(j, slot):
        q = q_ref[pl.ds(pl.multiple_of(j * Q_TILE, Q_TILE), Q_TILE), :]
        lane = lax.broadcasted_iota(jnp.int32, q.shape, 1)
        zero = jnp.zeros_like(q)
        qs = jnp.concatenate([jnp.where(lane < HEAD_DIM, q, zero),
                              jnp.where(lane >= HEAD_DIM, q, zero)], axis=0)
        m = None
        off = 0
        for kr, n in segs:
            for c in range(n // KEY_CHUNK):
                kc = kr[c * KEY_CHUNK:(c + 1) * KEY_CHUNK, :]
                s = lax.dot_general(kc, qs, (((1,), (1,)), ((), ())), preferred_element_type=F32)
                s_bufs[slot][off:off + KEY_CHUNK, :] = s
                mc = jnp.max(s, axis=0, keepdims=True)
                m = mc if m is None else jnp.maximum(m, mc)
                off += KEY_CHUNK
        return m

    def value_stage(j, slot, m):
        l = jnp.zeros((1, 2 * Q_TILE), F32)
        o12 = None
        off = 0
        for vr, n in vsegs:
            for c in range(n // KEY_CHUNK):
                p = jnp.exp2(s_bufs[slot][off:off + KEY_CHUNK, :] - m)
                l = l + jnp.sum(p, axis=0, keepdims=True)
                d = jnp.dot(vr[0, :, c * KEY_CHUNK:(c + 1) * KEY_CHUNK], p.astype(BF16),
                            preferred_element_type=F32)
                o12 = d if o12 is None else o12 + d
                off += KEY_CHUNK
        on = o12 * (1.0 / l)
        if diff:
            o = on[:, :Q_TILE] - lam * on[:, Q_TILE:]
            o = o * lax.rsqrt(jnp.mean(o * o, axis=0, keepdims=True) + LN_EPS) * gcol
        else:
            o = jnp.concatenate([on[:, :Q_TILE], on[:, Q_TILE:]], axis=0)
        o_ref[pl.ds(pl.multiple_of(j * Q_TILE, Q_TILE), Q_TILE), :] = o.T.astype(o_ref.dtype)

    def body(jj, m_even):
        j = 2 * jj
        m_odd = score_stage(j + 1, 1)
        value_stage(j, 0, m_even)
        m_even = score_stage(j + 2, 0)
        value_stage(j + 1, 1, m_odd)
        return m_even

    m_even = lax.fori_loop(0, n_sub // 2 - 1, body, score_stage(0, 0))
    m_odd = score_stage(n_sub - 1, 1)
    value_stage(n_sub - 2, 0, m_even)
    value_stage(n_sub - 1, 1, m_odd)


def _attention(q, k, vt, cache, diff_params, n_batch, seq, q_chunk, name):
    diff = diff_params is not None
    has_cache = cache is not None
    t = q.shape[0]
    nqc = seq // q_chunk
    dv = 128 if diff else 64
    kv_unit = (lambda u: u) if diff else (lambda u: u // 2)
    in_specs = [pl.BlockSpec((q_chunk, LANES), lambda b, u, i: (b * nqc + i, u)),
                pl.BlockSpec((seq, LANES), lambda b, u, i: (b, kv_unit(u))),
                pl.BlockSpec((1, dv, seq), lambda b, u, i: (b, kv_unit(u), 0))]
    args = [q, k, vt]
    total = seq
    if has_cache:
        kc, vtc = cache
        past = vtc.shape[2]
        total += past
        in_specs += [pl.BlockSpec((past, LANES), lambda b, u, i: (b, kv_unit(u))),
                     pl.BlockSpec((1, dv, past), lambda b, u, i: (b, kv_unit(u), 0))]
        args += [kc, vtc]
    if diff:
        lam_vecs, gcol = diff_params
        in_specs += [pl.BlockSpec((4, HEAD_DIM), lambda b, u, i: (0, 0)),
                     pl.BlockSpec((LANES, 1), lambda b, u, i: (0, 0))]
        args += [lam_vecs, gcol]
    return pl.pallas_call(
        functools.partial(_attn_kernel, diff=diff, has_cache=has_cache, n_keys=seq, q_chunk=q_chunk),
        out_shape=jax.ShapeDtypeStruct((t, 1024), BF16),
        grid=(n_batch, 8, nqc),
        in_specs=in_specs,
        out_specs=pl.BlockSpec((q_chunk, LANES), lambda b, u, i: (b * nqc + i, u)),
        scratch_shapes=[pltpu.VMEM((total, 2 * Q_TILE), F32),
                        pltpu.VMEM((total, 2 * Q_TILE), F32)],
        compiler_params=_cparams(("arbitrary", "arbitrary", "arbitrary")),
        name=name,
    )(*args)


def _split_bf16(x):
    hi = x.astype(BF16)
    lo = (x - hi.astype(F32)).astype(BF16)
    return hi, lo


def _post_kernel(x_ref, oa_ref, ob_ref, mod_ref, wg_ref, wa_ref, wb_ref, wo_ref, ln1g_ref, ln1b_ref,
                 wr_ref, br_ref, x1_ref, h2_ref, comb_ref):
    x = x_ref[...]
    mod = mod_ref[0]
    h = (_ln_plain(x) * (1.0 + mod[1:2, :]) + mod[0:1, :]).astype(BF16)
    gates = jnp.dot(h, wg_ref[...], preferred_element_type=F32)
    ma = jnp.dot(oa_ref[...], wa_ref[...], preferred_element_type=F32)
    mb = jnp.dot(ob_ref[...], wb_ref[...], preferred_element_type=F32)
    merged = jax.nn.sigmoid(gates[:, :1024]) * ma + jax.nn.sigmoid(gates[:, 1024:]) * mb
    mo = jnp.dot(merged.astype(BF16), wo_ref[...], preferred_element_type=F32)
    x1 = _ln_plain(DEEPNORM_ALPHA * x + mod[2:3, :] * mo) * ln1g_ref[...] + ln1b_ref[...]
    x1_ref[...] = x1
    h2 = _ln_plain(x1) * (1.0 + mod[4:5, :]) + mod[3:4, :]
    h2_ref[...] = h2.astype(BF16)

    nt = (((1,), (1,)), ((), ()))
    w_hi, w_lo = _split_bf16(wr_ref[...])
    h_hi, h_lo = _split_bf16(h2)
    lt = (lax.dot_general(w_hi, h_hi, nt, preferred_element_type=F32)
          + lax.dot_general(w_hi, h_lo, nt, preferred_element_type=F32)
          + lax.dot_general(w_lo, h_hi, nt, preferred_element_type=F32)) + br_ref[...]

    def top1(rows):
        best, idx = rows[0], jnp.zeros_like(rows[0], jnp.int32)
        for k in range(1, len(rows)):
            better = rows[k] > best
            best = jnp.where(better, rows[k], best)
            idx = jnp.where(better, k, idx)
        return best, idx

    coarse = [lt[g:g + 1, :] for g in range(N_GROUPS)]
    cmax, grp = top1(coarse)
    p_grp = 1.0 / sum(jnp.exp(cg - cmax) for cg in coarse)
    fine = []
    for e in range(EXPERTS_PER_GROUP):
        fe = lt[N_GROUPS + e:N_GROUPS + e + 1, :]
        for g in range(1, N_GROUPS):
            r = N_GROUPS + g * EXPERTS_PER_GROUP + e
            fe = jnp.where(grp == g, lt[r:r + 1, :], fe)
        fine.append(fe)
    fmax, e1 = top1(fine)
    ex = [jnp.exp(f - fmax) for f in fine]
    den = sum(ex)
    pf = [v / den for v in ex]
    neg = jnp.full_like(fmax, -jnp.inf)
    _, e2 = top1([jnp.where(e1 == e, neg, pf[e]) for e in range(EXPERTS_PER_GROUP)])
    p1 = sum(jnp.where(e1 == e, pf[e], 0.0) for e in range(EXPERTS_PER_GROUP))
    p2 = sum(jnp.where(e2 == e, pf[e], 0.0) for e in range(EXPERTS_PER_GROUP))
    wsum = p1 + p2
    rows = []
    for g in range(N_GROUPS):
        for e in range(EXPERTS_PER_GROUP):
            within = jnp.where(e1 == e, p1 / wsum, 0.0) + jnp.where(e2 == e, p2 / wsum, 0.0)
            rows.append(jnp.where(grp == g, p_grp * within, 0.0))
    comb_t = jnp.concatenate(rows + [jnp.zeros((LANES - N_EXPERTS, x.shape[0]), F32)], axis=0)
    comb_ref[...] = comb_t.T


def _post(x2d, oa, ob, mod, w_gate, w_a, w_b, w_o, ln1g, ln1b, wr_t, br_col, name):
    t = x2d.shape[0]
    tm = TOKEN_TILE
    tiles_per_batch = t // mod.shape[0] // tm
    row = lambda i: (i, 0)
    const = lambda i: (0, 0)
    return pl.pallas_call(
        _post_kernel,
        out_shape=[jax.ShapeDtypeStruct((t, D_MODEL), F32),
                   jax.ShapeDtypeStruct((t, D_MODEL), BF16),
                   jax.ShapeDtypeStruct((t, LANES), F32)],
        grid=(t // tm,),
        in_specs=[pl.BlockSpec((tm, D_MODEL), row),
                  pl.BlockSpec((tm, 1024), row),
                  pl.BlockSpec((tm, 1024), row),
                  pl.BlockSpec((1, 6, D_MODEL), lambda i: (i // tiles_per_batch, 0, 0)),
                  pl.BlockSpec((D_MODEL, GATE_W), const),
                  pl.BlockSpec((1024, D_MODEL), const),
                  pl.BlockSpec((1024, D_MODEL), const),
                  pl.BlockSpec((D_MODEL, D_MODEL), const),
                  pl.BlockSpec((1, D_MODEL), const),
                  pl.BlockSpec((1, D_MODEL), const),
                  pl.BlockSpec((LANES, D_MODEL), const),
                  pl.BlockSpec((LANES, 1), const)],
        out_specs=[pl.BlockSpec((tm, D_MODEL), row),
                   pl.BlockSpec((tm, D_MODEL), row),
                   pl.BlockSpec((tm, LANES), row)],
        compiler_params=_cparams(("arbitrary",)),
        name=name,
    )(x2d, oa, ob, mod, w_gate, w_a, w_b, w_o, ln1g, ln1b, wr_t, br_col)


def _moe_kernel(h_ref, comb_ref, wgu_ref, wd_ref, x1_ref, mod_ref, g_ref, b_ref, o_ref, acc_ref):
    e = pl.program_id(1)

    @pl.when(e == 0)
    def _():
        acc_ref[...] = jnp.zeros_like(acc_ref)

    gu = jnp.dot(h_ref[...], wgu_ref[0], preferred_element_type=F32)
    gg, uu = gu[:, :EXPERT_FF], gu[:, EXPERT_FF:]
    act = (gg * jax.nn.sigmoid(gg) * uu).astype(BF16)
    y = jnp.dot(act, wd_ref[0], preferred_element_type=F32)
    comb = comb_ref[...]
    lane = lax.broadcasted_iota(jnp.int32, comb.shape, 1)
    c = jnp.sum(jnp.where(lane == e, comb, 0.0), axis=1, keepdims=True)
    acc_ref[...] += c * y

    @pl.when(e == N_EXPERTS - 1)
    def _():
        mod = mod_ref[0]
        z = DEEPNORM_ALPHA * x1_ref[...] + mod[5:6, :] * acc_ref[...]
        o_ref[...] = _ln_plain(z) * g_ref[...] + b_ref[...]


def _moe(h2, comb, w_gu, w_d, x1, mod, ln2g, ln2b, name):
    t = h2.shape[0]
    tm = MOE_TILE
    tiles_per_batch = t // mod.shape[0] // tm
    row = lambda i, e: (i, 0)
    const = lambda i, e: (0, 0)
    return pl.pallas_call(
        _moe_kernel,
        out_shape=jax.ShapeDtypeStruct((t, D_MODEL), F32),
        grid=(t // tm, N_EXPERTS),
        in_specs=[pl.BlockSpec((tm, D_MODEL), row),
                  pl.BlockSpec((tm, LANES), row),
                  pl.BlockSpec((1, D_MODEL, 2 * EXPERT_FF), lambda i, e: (e, 0, 0)),
                  pl.BlockSpec((1, EXPERT_FF, D_MODEL), lambda i, e: (e, 0, 0)),
                  pl.BlockSpec((tm, D_MODEL), row),
                  pl.BlockSpec((1, 6, D_MODEL), lambda i, e: (i // tiles_per_batch, 0, 0)),
                  pl.BlockSpec((1, D_MODEL), const),
                  pl.BlockSpec((1, D_MODEL), const)],
        out_specs=pl.BlockSpec((tm, D_MODEL), row),
        scratch_shapes=[pltpu.VMEM((tm, D_MODEL), F32)],
        compiler_params=_cparams(("arbitrary", "arbitrary")),
        name=name,
    )(h2, comb, w_gu, w_d, x1, mod, ln2g, ln2b)


def _rope_tables(n_tokens):
    n_rows = n_tokens // GRID_W
    rows = jnp.repeat(jnp.arange(n_rows, dtype=F32), GRID_W)
    cols = jnp.tile(jnp.arange(GRID_W, dtype=F32), n_rows)
    axis_dim = HEAD_DIM // 2
    freqs = ROPE_THETA ** (-jnp.arange(0, axis_dim, 2, dtype=F32) / axis_dim)
    ang = jnp.concatenate([rows[:, None] * freqs, cols[:, None] * freqs], -1)
    ang = jnp.tile(jnp.repeat(ang, 2, axis=1), (1, LANES // HEAD_DIM))
    even = (jnp.arange(LANES) % 2 == 0)[None, :]
    cos, sin = jnp.cos(ang), jnp.sin(ang)
    return cos, jnp.where(even, -sin, 0.0), jnp.where(even, 0.0, sin)


def kernel(x_prompt, x_sample, cache_a_k, cache_a_v, cache_b_k, cache_b_v, c, c_ctx, w_ada, b_ada, w_in,
           lambda_q1, lambda_k1, lambda_q2, lambda_k2, a_subln_g, b_q_g, b_k_g, w_br_a, w_br_b, w_o,
           ln1_g, ln1_b, w_coarse, b_coarse, w_fine, b_fine, w_gate_up, w_down, ln2_g, ln2_b):
    n_ctx, s_ctx, d = x_prompt.shape
    n_lat, s_lat, _ = x_sample.shape
    past = cache_a_k.shape[2]

    w_qkv = w_in[0, :, :QKV_W].astype(BF16)
    w_gate = w_in[0, :, QKV_W:].astype(BF16)
    w_a, w_b, w_out = w_br_a[0].astype(BF16), w_br_b[0].astype(BF16), w_o[0].astype(BF16)
    w_gu, w_d = w_gate_up[0].astype(BF16), w_down[0].astype(BF16)
    gq = jnp.tile(b_q_g[0], B_HEADS)[None, :]
    gk = jnp.tile(b_k_g[0], B_KV_HEADS)[None, :]
    blk = jnp.arange(MXU_DIM) // HEAD_DIM
    bd = jnp.where(blk[:, None] == blk[None, :], 1.0 / HEAD_DIM, 0.0).astype(BF16)
    lam_vecs = jnp.concatenate([lambda_q1, lambda_k1, lambda_q2, lambda_k2], axis=0)
    gcol = a_subln_g[0][:, None]
    wr_t = jnp.zeros((LANES, d), F32).at[:N_GROUPS].set(w_coarse[0].T).at[N_GROUPS:N_GROUPS + N_EXPERTS].set(w_fine[0].T)
    br_col = jnp.zeros((LANES, 1), F32).at[:N_GROUPS, 0].set(b_coarse[0]).at[N_GROUPS:N_GROUPS + N_EXPERTS, 0].set(b_fine[0])
    ln1g, ln1b, ln2g, ln2b = ln1_g, ln1_b, ln2_g, ln2_b

    cond = jnp.concatenate([c_ctx[None, :], c], axis=0)
    mod = _modulation(cond, w_ada[0], b_ada).reshape(1 + n_lat, 6, d)
    mod_ctx, mod_lat = mod[:1], mod[1:]

    cak = cache_a_k[:, 0].reshape(n_lat * past, 1024).astype(BF16)
    cavt = jnp.swapaxes(cache_a_v[:, 0].reshape(n_lat, past, 1024), 1, 2).astype(BF16)
    cbk = jnp.repeat(cache_b_k[:, 0], 2, axis=2).reshape(n_lat * past, 512).astype(BF16)
    cbvt = jnp.swapaxes(cache_b_v[:, 0].reshape(n_lat, past, 256), 1, 2).astype(BF16)

    diff_params = (lam_vecs, gcol)

    def tail(x2d, oa, ob, mod_x, tag):
        x1, h2, comb = _post(x2d, oa, ob, mod_x, w_gate, w_a, w_b, w_out, ln1g, ln1b, wr_t, br_col, "post_" + tag)
        return _moe(h2, comb, w_gu, w_d, x1, mod_x, ln2g, ln2b, "moe_" + tag)

    xp = x_prompt.reshape(n_ctx * s_ctx, d)
    qa, ka, vat, qb, kbd, vbt, nak, nav, nbk, nbv = _inproj(xp, mod_ctx, w_qkv, gq, gk, bd, None, n_ctx, s_ctx, True)
    oa = _attention(qa, ka, vat, None, diff_params, n_ctx, s_ctx, s_ctx, "attn_a_ctx")
    ob = _attention(qb, kbd, vbt, None, None, n_ctx, s_ctx, s_ctx, "attn_b_ctx")
    y_prompt = tail(xp, oa, ob, mod_ctx, "ctx").reshape(n_ctx, s_ctx, d)

    xs = x_sample.reshape(n_lat * s_lat, d)
    qa, ka, vat, qb, kbd, vbt = _inproj(xs, mod_lat, w_qkv, gq, gk, bd, _rope_tables(s_lat), n_lat, s_lat, False)
    oa = _attention(qa, ka, vat, (cak, cavt), diff_params, n_lat, s_lat, s_lat, "attn_a_lat")
    ob = _attention(qb, kbd, vbt, (cbk, cbvt), None, n_lat, s_lat, s_lat, "attn_b_lat")
    y_sample = tail(xs, oa, ob, mod_lat, "lat").reshape(n_lat, s_lat, d)

    return (y_prompt, y_sample,
            nak.reshape(n_ctx, 1, s_ctx, A_HEADS, 2, HEAD_DIM),
            nav.reshape(n_ctx, 1, s_ctx, A_HEADS, 2 * HEAD_DIM),
            nbk.reshape(n_ctx, 1, s_ctx, B_KV_HEADS, HEAD_DIM),
            nbv.reshape(n_ctx, 1, s_ctx, B_KV_HEADS, HEAD_DIM))
```

```python
import functools
import math

import jax
import jax.numpy as jnp
from jax import lax
from jax.experimental import pallas as pl
from jax.experimental.pallas import tpu as pltpu

F32 = jnp.float32
BF16 = jnp.bfloat16

D_MODEL = 1024
GRID_W = 64
ROPE_THETA = 10000.0
LN_EPS = 1e-5
HEAD_DIM = 64
A_HEADS = 8
B_HEADS = 16
B_KV_HEADS = 4
N_GROUPS = 4
EXPERTS_PER_GROUP = 4
N_EXPERTS = 16
EXPERT_FF = 512
DEPTH = 1
DEEPNORM_ALPHA = (2 * DEPTH) ** 0.25
LAMBDA_INIT = 0.8 - 0.6 * math.exp(-0.3 * 0)
QKV_W = 4608
GATE_W = 2048
LANES = 128
MXU_DIM = 256
Q_SCALE = HEAD_DIM ** -0.5 * math.log2(math.e)
VMEM_LIMIT = 56 * 1024 * 1024

TOKEN_TILE = 256
Q_TILE = 128
KEY_CHUNK = 256
MOE_BLOCK = 256
PAIRS_PER_GROUP = 6
N_CLASSES = N_GROUPS * PAIRS_PER_GROUP
CLASS_ROWS = 32
H2X_W = D_MODEL + LANES


def _cparams(sem):
    return pltpu.CompilerParams(dimension_semantics=sem, vmem_limit_bytes=VMEM_LIMIT)


def _ln_plain(x):
    mu = jnp.mean(x, axis=-1, keepdims=True)
    xc = x - mu
    var = jnp.mean(xc * xc, axis=-1, keepdims=True)
    return xc * lax.rsqrt(var + LN_EPS)


def _mod_kernel(c_ref, w_ref, b_ref, o_ref):
    c = c_ref[...]
    s = (c * jax.nn.sigmoid(c)).astype(BF16)
    o_ref[...] = jnp.dot(s, w_ref[...].astype(BF16), preferred_element_type=F32) + b_ref[...]


def _modulation(cond, w_ada, b_ada):
    n, d = cond.shape
    tn = 1536
    return pl.pallas_call(
        _mod_kernel,
        out_shape=jax.ShapeDtypeStruct((n, 6 * d), F32),
        grid=(6 * d // tn,),
        in_specs=[pl.BlockSpec((n, d), lambda j: (0, 0)),
                  pl.BlockSpec((d, tn), lambda j: (0, j)),
                  pl.BlockSpec((1, tn), lambda j: (0, j))],
        out_specs=pl.BlockSpec((n, tn), lambda j: (0, j)),
        compiler_params=_cparams(("arbitrary",)),
        name="adaln_mod",
    )(cond, w_ada, b_ada)


def _rope(x, c, se, so):
    outs = []
    for j in range(x.shape[1] // LANES):
        xc = x[:, j * LANES:(j + 1) * LANES]
        outs.append(xc * c + pltpu.roll(xc, LANES - 1, 1) * se + pltpu.roll(xc, 1, 1) * so)
    return outs[0] if len(outs) == 1 else jnp.concatenate(outs, axis=1)


def _head_rms(x, bd):
    outs = []
    for j in range(x.shape[1] // MXU_DIM):
        xc = x[:, j * MXU_DIM:(j + 1) * MXU_DIM]
        ms = jnp.dot((xc * xc).astype(BF16), bd, preferred_element_type=F32)
        outs.append(xc * lax.rsqrt(ms + LN_EPS))
    return outs[0] if len(outs) == 1 else jnp.concatenate(outs, axis=1)


def _dup_heads(x):
    outs = []
    lane = lax.broadcasted_iota(jnp.int32, (x.shape[0], LANES), 1)
    lo = lane < HEAD_DIM
    for j in range(x.shape[1] // LANES):
        xc = x[:, j * LANES:(j + 1) * LANES]
        r = pltpu.roll(xc, HEAD_DIM, 1)
        outs.append(jnp.where(lo, xc, r))
        outs.append(jnp.where(lo, r, xc))
    return jnp.concatenate(outs, axis=1)


def _inproj_kernel(*refs, use_rope, emit_ctx):
    it = iter(refs)
    x_ref, mod_ref, w_ref, gq_ref, gk_ref, bd_ref = (next(it) for _ in range(6))
    if use_rope:
        c_ref, se_ref, so_ref = (next(it) for _ in range(3))
    qa_ref, ka_ref, vat_ref, qb_ref, kbd_ref, vbt_ref = (next(it) for _ in range(6))
    if emit_ctx:
        nak_ref, nav_ref, nbk_ref, nbv_ref = (next(it) for _ in range(4))

    x = x_ref[...]
    mod = mod_ref[0]
    h = (_ln_plain(x) * (1.0 + mod[1:2, :]) + mod[0:1, :]).astype(BF16)
    bd = bd_ref[...]
    if use_rope:
        c, se, so = c_ref[...], se_ref[...], so_ref[...]

    def proj(lo, hi):
        return jnp.dot(h, w_ref[:, lo:hi], preferred_element_type=F32)

    aq = proj(0, 1024)
    if use_rope:
        aq = _rope(aq, c, se, so)
    qa_ref[...] = (aq * Q_SCALE).astype(BF16)

    ak = proj(1024, 2048)
    if emit_ctx:
        nak_ref[...] = ak
    if use_rope:
        ak = _rope(ak, c, se, so)
    ka_ref[...] = ak.astype(BF16)

    av = proj(2048, 3072)
    if emit_ctx:
        nav_ref[...] = av
    vat_ref[0] = av.T.astype(BF16)

    bq = _head_rms(proj(3072, 4096), bd) * gq_ref[...]
    if use_rope:
        bq = _rope(bq, c, se, so)
    qb_ref[...] = (bq * Q_SCALE).astype(BF16)

    bk = _head_rms(proj(4096, 4352), bd) * gk_ref[...]
    if emit_ctx:
        nbk_ref[...] = bk
    if use_rope:
        bk = _rope(bk, c, se, so)
    kbd_ref[...] = _dup_heads(bk).astype(BF16)

    bv = proj(4352, 4608)
    if emit_ctx:
        nbv_ref[...] = bv
    vbt_ref[0] = bv.T.astype(BF16)


def _inproj(x2d, mod, w_qkv, gq, gk, bd, rope, n_batch, seq, emit_ctx):
    t = x2d.shape[0]
    tm = TOKEN_TILE
    tiles_per_batch = seq // tm
    tiles_per_mod = t // mod.shape[0] // tm
    use_rope = rope is not None
    row = lambda i: (i, 0)
    const = lambda i: (0, 0)
    in_specs = [pl.BlockSpec((tm, D_MODEL), row),
                pl.BlockSpec((1, 6, D_MODEL), lambda i: (i // tiles_per_mod, 0, 0)),
                pl.BlockSpec((D_MODEL, QKV_W), const),
                pl.BlockSpec((1, 1024), const),
                pl.BlockSpec((1, 256), const),
                pl.BlockSpec((MXU_DIM, MXU_DIM), const)]
    args = [x2d, mod, w_qkv, gq, gk, bd]
    if use_rope:
        tab = pl.BlockSpec((tm, LANES), lambda i: (i % tiles_per_batch, 0))
        in_specs += [tab, tab, tab]
        args += list(rope)
    vt_idx = lambda i: (i // tiles_per_batch, 0, i % tiles_per_batch)
    out_shape = [jax.ShapeDtypeStruct((t, 1024), BF16),
                 jax.ShapeDtypeStruct((t, 1024), BF16),
                 jax.ShapeDtypeStruct((n_batch, 1024, seq), BF16),
                 jax.ShapeDtypeStruct((t, 1024), BF16),
                 jax.ShapeDtypeStruct((t, 512), BF16),
                 jax.ShapeDtypeStruct((n_batch, 256, seq), BF16)]
    out_specs = [pl.BlockSpec((tm, 1024), row),
                 pl.BlockSpec((tm, 1024), row),
                 pl.BlockSpec((1, 1024, tm), vt_idx),
                 pl.BlockSpec((tm, 1024), row),
                 pl.BlockSpec((tm, 512), row),
                 pl.BlockSpec((1, 256, tm), vt_idx)]
    if emit_ctx:
        out_shape += [jax.ShapeDtypeStruct((t, 1024), F32),
                      jax.ShapeDtypeStruct((t, 1024), F32),
                      jax.ShapeDtypeStruct((t, 256), F32),
                      jax.ShapeDtypeStruct((t, 256), F32)]
        out_specs += [pl.BlockSpec((tm, 1024), row),
                      pl.BlockSpec((tm, 1024), row),
                      pl.BlockSpec((tm, 256), row),
                      pl.BlockSpec((tm, 256), row)]
    return pl.pallas_call(
        functools.partial(_inproj_kernel, use_rope=use_rope, emit_ctx=emit_ctx),
        out_shape=out_shape,
        grid=(t // tm,),
        in_specs=in_specs,
        out_specs=out_specs,
        compiler_params=_cparams(("arbitrary",)),
        name="inproj_ctx" if emit_ctx else "inproj_lat",
    )(*args)


def _attn_kernel(*refs, diff, has_cache, n_keys, q_chunk):
    it = iter(refs)
    q_ref, k_ref, vt_ref = (next(it) for _ in range(3))
    if has_cache:
        kc_ref, vtc_ref = (next(it) for _ in range(2))
    if diff:
        lam_ref, g_ref = (next(it) for _ in range(2))
    o_ref, s0_ref, s1_ref = (next(it) for _ in range(3))
    s_bufs = (s0_ref, s1_ref)

    segs = [(k_ref, n_keys)]
    vsegs = [(vt_ref, n_keys)]
    if has_cache:
        segs.append((kc_ref, kc_ref.shape[0]))
        vsegs.append((vtc_ref, kc_ref.shape[0]))

    if diff:
        lv = lam_ref[...]
        lam = (jnp.exp(jnp.sum(lv[0:1] * lv[1:2], axis=1, keepdims=True))
               - jnp.exp(jnp.sum(lv[2:3] * lv[3:4], axis=1, keepdims=True)) + LAMBDA_INIT)
        gcol = g_ref[...] * (1.0 - LAMBDA_INIT)

    n_sub = q_chunk // Q_TILE

    def score_stage(j, slot):
        q = q_ref[pl.ds(pl.multiple_of(j * Q_TILE, Q_TILE), Q_TILE), :]
        lane = lax.broadcasted_iota(jnp.int32, q.shape, 1)
        zero = jnp.zeros_like(q)
        qs = jnp.concatenate([jnp.where(lane < HEAD_DIM, q, zero),
                              jnp.where(lane >= HEAD_DIM, q, zero)], axis=0)
        m = None
        off = 0
        for kr, n in segs:
            for c in range(n // KEY_CHUNK):
                kc = kr[c * KEY_CHUNK:(c + 1) * KEY_CHUNK, :]
                s = lax.dot_general(kc, qs, (((1,), (1,)), ((), ())), preferred_element_type=F32)
                s_bufs[slot][off:off + KEY_CHUNK, :] = s
                mc = jnp.max(s, axis=0, keepdims=True)
                m = mc if m is None else jnp.maximum(m, mc)
                off += KEY_CHUNK
        return m

    def value_stage(j, slot, m):
        l = jnp.zeros((1, 2 * Q_TILE), F32)
        o12 = None
        off = 0
        for vr, n in vsegs:
            for c in range(n // KEY_CHUNK):
                p = jnp.exp2(s_bufs[slot][off:off + KEY_CHUNK, :] - m)
                l = l + jnp.sum(p, axis=0, keepdims=True)
                d = jnp.dot(vr[0, :, c * KEY_CHUNK:(c + 1) * KEY_CHUNK], p.astype(BF16),
                            preferred_element_type=F32)
                o12 = d if o12 is None else o12 + d
                off += KEY_CHUNK
        on = o12 * (1.0 / l)
        if diff:
            o = on[:, :Q_TILE] - lam * on[:, Q_TILE:]
            o = o * lax.rsqrt(jnp.mean(o * o, axis=0, keepdims=True) + LN_EPS) * gcol
        else:
            o = jnp.concatenate([on[:, :Q_TILE], on[:, Q_TILE:]], axis=0)
        o_ref[pl.ds(pl.multiple_of(j * Q_TILE, Q_TILE), Q_TILE), :] = o.T.astype(o_ref.dtype)

    def body(jj, m_even):
        j = 2 * jj
        m_odd = score_stage(j + 1, 1)
        value_stage(j, 0, m_even)
        m_even = score_stage(j + 2, 0)
        value_stage(j + 1, 1, m_odd)
        return m_even

    m_even = lax.fori_loop(0, n_sub // 2 - 1, body, score_stage(0, 0))
    m_odd = score_stage(n_sub - 1, 1)
    value_stage(n_sub - 2, 0, m_even)
    value_stage(n_sub - 1, 1, m_odd)


def _attention(q, k, vt, cache, diff_params, n_batch, seq, q_chunk, name):
    diff = diff_params is not None
    has_cache = cache is not None
    t = q.shape[0]
    nqc = seq // q_chunk
    dv = 128 if diff else 64
    kv_unit = (lambda u: u) if diff else (lambda u: u // 2)
    in_specs = [pl.BlockSpec((q_chunk, LANES), lambda b, u, i: (b * nqc + i, u)),
                pl.BlockSpec((seq, LANES), lambda b, u, i: (b, kv_unit(u))),
                pl.BlockSpec((1, dv, seq), lambda b, u, i: (b, kv_unit(u), 0))]
    args = [q, k, vt]
    total = seq
    if has_cache:
        kc, vtc = cache
        past = vtc.shape[2]
        total += past
        in_specs += [pl.BlockSpec((past, LANES), lambda b, u, i: (b, kv_unit(u))),
                     pl.BlockSpec((1, dv, past), lambda b, u, i: (b, kv_unit(u), 0))]
        args += [kc, vtc]
    if diff:
        lam_vecs, gcol = diff_params
        in_specs += [pl.BlockSpec((4, HEAD_DIM), lambda b, u, i: (0, 0)),
                     pl.BlockSpec((LANES, 1), lambda b, u, i: (0, 0))]
        args += [lam_vecs, gcol]
    return pl.pallas_call(
        functools.partial(_attn_kernel, diff=diff, has_cache=has_cache, n_keys=seq, q_chunk=q_chunk),
        out_shape=jax.ShapeDtypeStruct((t, 1024), BF16),
        grid=(n_batch, 8, nqc),
        in_specs=in_specs,
        out_specs=pl.BlockSpec((q_chunk, LANES), lambda b, u, i: (b * nqc + i, u)),
        scratch_shapes=[pltpu.VMEM((total, 2 * Q_TILE), F32),
                        pltpu.VMEM((total, 2 * Q_TILE), F32)],
        compiler_params=_cparams(("arbitrary", "arbitrary", "arbitrary")),
        name=name,
    )(*args)


def _split_bf16(x):
    hi = x.astype(BF16)
    lo = (x - hi.astype(F32)).astype(BF16)
    return hi, lo


def _post_kernel(x_ref, oa_ref, ob_ref, mod_ref, wg_ref, wa_ref, wb_ref, wo_ref, ln1g_ref, ln1b_ref,
                 wr_ref, br_ref, tri_ref, x1_ref, h2x_ref, cls_ref, rank_ref, cnt_ref, carry_ref):
    @pl.when(pl.program_id(0) == 0)
    def _():
        carry_ref[...] = jnp.zeros_like(carry_ref)

    x = x_ref[...]
    mod = mod_ref[0]
    h = (_ln_plain(x) * (1.0 + mod[1:2, :]) + mod[0:1, :]).astype(BF16)
    gates = jnp.dot(h, wg_ref[...], preferred_element_type=F32)
    ma = jnp.dot(oa_ref[...], wa_ref[...], preferred_element_type=F32)
    mb = jnp.dot(ob_ref[...], wb_ref[...], preferred_element_type=F32)
    merged = jax.nn.sigmoid(gates[:, :1024]) * ma + jax.nn.sigmoid(gates[:, 1024:]) * mb
    mo = jnp.dot(merged.astype(BF16), wo_ref[...], preferred_element_type=F32)
    x1 = _ln_plain(DEEPNORM_ALPHA * x + mod[2:3, :] * mo) * ln1g_ref[...] + ln1b_ref[...]
    x1_ref[...] = x1
    h2 = _ln_plain(x1) * (1.0 + mod[4:5, :]) + mod[3:4, :]
    h2x_ref[:, :D_MODEL] = h2

    nt = (((1,), (1,)), ((), ()))
    w_hi, w_lo = _split_bf16(wr_ref[...])
    h_hi, h_lo = _split_bf16(h2)
    lt = (lax.dot_general(w_hi, h_hi, nt, preferred_element_type=F32)
          + lax.dot_general(w_hi, h_lo, nt, preferred_element_type=F32)
          + lax.dot_general(w_lo, h_hi, nt, preferred_element_type=F32)) + br_ref[...]

    def top1(rows):
        best, idx = rows[0], jnp.zeros_like(rows[0], jnp.int32)
        for k in range(1, len(rows)):
            better = rows[k] > best
            best = jnp.where(better, rows[k], best)
            idx = jnp.where(better, k, idx)
        return best, idx

    coarse = [lt[g:g + 1, :] for g in range(N_GROUPS)]
    cmax, grp = top1(coarse)
    p_grp = 1.0 / sum(jnp.exp(cg - cmax) for cg in coarse)
    fine = []
    for e in range(EXPERTS_PER_GROUP):
        fe = lt[N_GROUPS + e:N_GROUPS + e + 1, :]
        for g in range(1, N_GROUPS):
            r = N_GROUPS + g * EXPERTS_PER_GROUP + e
            fe = jnp.where(grp == g, lt[r:r + 1, :], fe)
        fine.append(fe)
    fmax, e1 = top1(fine)
    ex = [jnp.exp(f - fmax) for f in fine]
    den = sum(ex)
    pf = [v / den for v in ex]
    neg = jnp.full_like(fmax, -jnp.inf)
    _, e2 = top1([jnp.where(e1 == e, neg, pf[e]) for e in range(EXPERTS_PER_GROUP)])
    p1 = sum(jnp.where(e1 == e, pf[e], 0.0) for e in range(EXPERTS_PER_GROUP))
    p2 = sum(jnp.where(e2 == e, pf[e], 0.0) for e in range(EXPERTS_PER_GROUP))
    wsum = p1 + p2
    c1, c2 = p_grp * (p1 / wsum), p_grp * (p2 / wsum)
    e_lo, e_hi = jnp.minimum(e1, e2), jnp.maximum(e1, e2)
    pair = jnp.where(e_lo == 0, e_hi - 1, jnp.where(e_lo == 1, e_hi + 1, PAIRS_PER_GROUP - 1))
    cls = grp * PAIRS_PER_GROUP + pair
    w_first = jnp.where(e1 < e2, c1, c2)
    w_second = jnp.where(e1 < e2, c2, c1)
    tm = x.shape[0]
    side = jnp.concatenate([w_first, w_second, jnp.zeros((LANES - 2, tm), F32)], axis=0)
    h2x_ref[:, D_MODEL:] = side.T
    cls_ref[...] = cls

    onehot = lax.broadcasted_iota(jnp.int32, (CLASS_ROWS, tm), 0) == cls
    cum = jnp.dot(onehot.astype(BF16), tri_ref[...], preferred_element_type=F32)
    carry = carry_ref[...]
    rank = jnp.sum(jnp.where(onehot, cum - 1.0 + carry[:, 0:1], 0.0), axis=0, keepdims=True)
    rank_ref[...] = rank.astype(jnp.int32)
    carry = carry + cum[:, tm - 1:tm]
    carry_ref[...] = carry
    cnt_ref[...] = carry


def _post(x2d, oa, ob, mod, w_gate, w_a, w_b, w_o, ln1g, ln1b, wr_t, br_col, tri, name):
    t = x2d.shape[0]
    tm = TOKEN_TILE
    tiles_per_batch = t // mod.shape[0] // tm
    row = lambda i: (i, 0)
    const = lambda i: (0, 0)
    return pl.pallas_call(
        _post_kernel,
        out_shape=[jax.ShapeDtypeStruct((t, D_MODEL), F32),
                   jax.ShapeDtypeStruct((t, H2X_W), F32),
                   jax.ShapeDtypeStruct((1, t), jnp.int32),
                   jax.ShapeDtypeStruct((1, t), jnp.int32),
                   jax.ShapeDtypeStruct((CLASS_ROWS, LANES), F32)],
        grid=(t // tm,),
        in_specs=[pl.BlockSpec((tm, D_MODEL), row),
                  pl.BlockSpec((tm, 1024), row),
                  pl.BlockSpec((tm, 1024), row),
                  pl.BlockSpec((1, 6, D_MODEL), lambda i: (i // tiles_per_batch, 0, 0)),
                  pl.BlockSpec((D_MODEL, GATE_W), const),
                  pl.BlockSpec((1024, D_MODEL), const),
                  pl.BlockSpec((1024, D_MODEL), const),
                  pl.BlockSpec((D_MODEL, D_MODEL), const),
                  pl.BlockSpec((1, D_MODEL), const),
                  pl.BlockSpec((1, D_MODEL), const),
                  pl.BlockSpec((LANES, D_MODEL), const),
                  pl.BlockSpec((LANES, 1), const),
                  pl.BlockSpec((tm, tm), const)],
        out_specs=[pl.BlockSpec((tm, D_MODEL), row),
                   pl.BlockSpec((tm, H2X_W), row),
                   pl.BlockSpec((1, tm), lambda i: (0, i)),
                   pl.BlockSpec((1, tm), lambda i: (0, i)),
                   pl.BlockSpec((CLASS_ROWS, LANES), const)],
        scratch_shapes=[pltpu.VMEM((CLASS_ROWS, LANES), F32)],
        compiler_params=_cparams(("arbitrary",)),
        name=name,
    )(x2d, oa, ob, mod, w_gate, w_a, w_b, w_o, ln1g, ln1b, wr_t, br_col, tri)


def _row_copy(src_hbm, dst_hbm, src_row, dst_row, sem):
    return pltpu.make_async_copy(src_hbm.at[pl.ds(src_row, 1)], dst_hbm.at[pl.ds(dst_row, 1)], sem)


def _moe_kernel(elo_ref, ehi_ref, nv_ref, nb_ref, src_ref, srcn_ref, dst_ref, h_hbm, wgu_lo_ref, wd_lo_ref,
                wgu_hi_ref, wd_hi_ref, y_hbm, xbuf, ybuf, gsem, ssem):
    s = pl.program_id(0)
    nb = nb_ref[0]
    br = MOE_BLOCK

    def wait_gather(slot):
        pltpu.make_async_copy(h_hbm.at[pl.ds(0, br)], xbuf.at[slot], gsem.at[slot]).wait()

    def scatter_rows(slot, n_valid, wait):
        def one(r, c):
            cp = _row_copy(ybuf.at[slot], y_hbm, r, dst_ref[0, 0, r], ssem.at[slot])
            cp.wait() if wait else cp.start()
            return c

        @pl.when(n_valid == br)
        def _():
            if wait:
                pltpu.make_async_copy(ybuf.at[slot], y_hbm.at[pl.ds(0, br)], ssem.at[slot]).wait()
            else:
                lax.fori_loop(0, br, one, 0, unroll=8)

        @pl.when(n_valid != br)
        def _():
            lax.fori_loop(0, n_valid, one, 0)

    def expert(xb, wgu_ref, wd_ref):
        gu = jnp.dot(xb, wgu_ref[0], preferred_element_type=F32)
        gg, uu = gu[:, :EXPERT_FF], gu[:, EXPERT_FF:]
        act = (gg * jax.nn.sigmoid(gg) * uu).astype(BF16)
        return jnp.dot(act, wd_ref[0], preferred_element_type=F32)

    def step(slot):
        @pl.when(s == 0)
        def _():
            def issue(r, c):
                _row_copy(h_hbm, xbuf.at[slot], src_ref[0, 0, r], r, gsem.at[slot]).start()
                return c
            lax.fori_loop(0, br, issue, 0, unroll=8)

        for r in range(br):
            _row_copy(h_hbm, xbuf.at[1 - slot], srcn_ref[0, 0, r], r, gsem.at[1 - slot]).start()
        wait_gather(slot)
        xb = xbuf[slot, :, :D_MODEL].astype(BF16)
        side = xbuf[slot, :, D_MODEL:]
        y = side[:, 0:1] * expert(xb, wgu_lo_ref, wd_lo_ref) + side[:, 1:2] * expert(xb, wgu_hi_ref, wd_hi_ref)
        ybuf[slot] = y
        scatter_rows(slot, nv_ref[s], wait=False)

        @pl.when(s >= 1)
        def _():
            scatter_rows(1 - slot, nv_ref[jnp.maximum(s - 1, 0)], wait=True)

        @pl.when(s == nb - 1)
        def _():
            scatter_rows(slot, nv_ref[s], wait=True)
            wait_gather(1 - slot)

    for parity in range(2):
        @pl.when(jnp.logical_and(s < nb, s % 2 == parity))
        def _(parity=parity):
            step(parity)


def _moe(h2x, src, dst, blk_elo, blk_ehi, blk_valid, nb_used, w_gu, w_d, name):
    nblk = src.shape[0]
    br = MOE_BLOCK
    smem_blk = lambda f: pl.BlockSpec((1, 1, br), f, memory_space=pltpu.SMEM)
    grid_spec = pltpu.PrefetchScalarGridSpec(
        num_scalar_prefetch=4,
        grid=(nblk,),
        in_specs=[smem_blk(lambda s, lo, hi, nv, nb: (s, 0, 0)),
                  smem_blk(lambda s, lo, hi, nv, nb: (jnp.minimum(s + 1, nblk - 1), 0, 0)),
                  smem_blk(lambda s, lo, hi, nv, nb: (s, 0, 0)),
                  pl.BlockSpec(memory_space=pl.ANY),
                  pl.BlockSpec((1, D_MODEL, 2 * EXPERT_FF), lambda s, lo, hi, nv, nb: (lo[s], 0, 0)),
                  pl.BlockSpec((1, EXPERT_FF, D_MODEL), lambda s, lo, hi, nv, nb: (lo[s], 0, 0)),
                  pl.BlockSpec((1, D_MODEL, 2 * EXPERT_FF), lambda s, lo, hi, nv, nb: (hi[s], 0, 0)),
                  pl.BlockSpec((1, EXPERT_FF, D_MODEL), lambda s, lo, hi, nv, nb: (hi[s], 0, 0))],
        out_specs=pl.BlockSpec(memory_space=pl.ANY),
        scratch_shapes=[pltpu.VMEM((2, br, H2X_W), F32),
                        pltpu.VMEM((2, br, D_MODEL), F32),
                        pltpu.SemaphoreType.DMA((2,)),
                        pltpu.SemaphoreType.DMA((2,))])
    return pl.pallas_call(
        _moe_kernel,
        out_shape=jax.ShapeDtypeStruct((h2x.shape[0], D_MODEL), F32),
        grid_spec=grid_spec,
        compiler_params=_cparams(("arbitrary",)),
        name=name,
    )(blk_elo, blk_ehi, blk_valid, nb_used, src, src, dst, h2x, w_gu, w_d, w_gu, w_d)


def _route_plan(cls, rank, counts, n_tokens):
    br = MOE_BLOCK
    nblk = n_tokens // br + N_CLASSES
    cnt = counts[:N_CLASSES, 0].astype(jnp.int32)
    padded = (cnt + br - 1) // br * br
    ends = jnp.cumsum(padded)
    starts = ends - padded
    slot_of_token = starts[cls[0]] + rank[0]
    token_of_slot = jnp.zeros((nblk * br,), jnp.int32).at[slot_of_token].set(jnp.arange(n_tokens, dtype=jnp.int32))
    nb_used = ends[-1] // br
    blk_start = jnp.minimum(jnp.arange(nblk, dtype=jnp.int32), nb_used - 1) * br
    blk_cls = jnp.searchsorted(ends, blk_start, side="right").astype(jnp.int32)
    blk_valid = jnp.clip(starts[blk_cls] + cnt[blk_cls] - blk_start, 0, br).astype(jnp.int32)
    grp, pair = blk_cls // PAIRS_PER_GROUP, blk_cls % PAIRS_PER_GROUP
    pair_lo = jnp.array([0, 0, 0, 1, 1, 2], jnp.int32)[pair]
    pair_hi = jnp.array([1, 2, 3, 2, 3, 3], jnp.int32)[pair]
    rows = token_of_slot.reshape(nblk, 1, br)
    return (rows, grp * EXPERTS_PER_GROUP + pair_lo, grp * EXPERTS_PER_GROUP + pair_hi, blk_valid,
            nb_used.reshape(1).astype(jnp.int32))


def _final_kernel(x1_ref, y_ref, mod_ref, g_ref, b_ref, o_ref):
    z = DEEPNORM_ALPHA * x1_ref[...] + mod_ref[0][5:6, :] * y_ref[...]
    o_ref[...] = _ln_plain(z) * g_ref[...] + b_ref[...]


def _final(x1, y_rows, mod, ln2g, ln2b, name):
    t = x1.shape[0]
    tm = 512
    tiles_per_batch = t // mod.shape[0] // tm
    row = lambda i: (i, 0)
    const = lambda i: (0, 0)
    return pl.pallas_call(
        _final_kernel,
        out_shape=jax.ShapeDtypeStruct((t, D_MODEL), F32),
        grid=(t // tm,),
        in_specs=[pl.BlockSpec((tm, D_MODEL), row),
                  pl.BlockSpec((tm, D_MODEL), row),
                  pl.BlockSpec((1, 6, D_MODEL), lambda i: (i // tiles_per_batch, 0, 0)),
                  pl.BlockSpec((1, D_MODEL), const),
                  pl.BlockSpec((1, D_MODEL), const)],
        out_specs=pl.BlockSpec((tm, D_MODEL), row),
        compiler_params=_cparams(("arbitrary",)),
        name=name,
    )(x1, y_rows, mod, ln2g, ln2b)


def _rope_tables(n_tokens):
    n_rows = n_tokens // GRID_W
    rows = jnp.repeat(jnp.arange(n_rows, dtype=F32), GRID_W)
    cols = jnp.tile(jnp.arange(GRID_W, dtype=F32), n_rows)
    axis_dim = HEAD_DIM // 2
    freqs = ROPE_THETA ** (-jnp.arange(0, axis_dim, 2, dtype=F32) / axis_dim)
    ang = jnp.concatenate([rows[:, None] * freqs, cols[:, None] * freqs], -1)
    ang = jnp.tile(jnp.repeat(ang, 2, axis=1), (1, LANES // HEAD_DIM))
    even = (jnp.arange(LANES) % 2 == 0)[None, :]
    cos, sin = jnp.cos(ang), jnp.sin(ang)
    return cos, jnp.where(even, -sin, 0.0), jnp.where(even, 0.0, sin)


def kernel(x_prompt, x_sample, cache_a_k, cache_a_v, cache_b_k, cache_b_v, c, c_ctx, w_ada, b_ada, w_in,
           lambda_q1, lambda_k1, lambda_q2, lambda_k2, a_subln_g, b_q_g, b_k_g, w_br_a, w_br_b, w_o,
           ln1_g, ln1_b, w_coarse, b_coarse, w_fine, b_fine, w_gate_up, w_down, ln2_g, ln2_b):
    n_ctx, s_ctx, d = x_prompt.shape
    n_lat, s_lat, _ = x_sample.shape
    past = cache_a_k.shape[2]

    w_qkv = w_in[0, :, :QKV_W].astype(BF16)
    w_gate = w_in[0, :, QKV_W:].astype(BF16)
    w_a, w_b, w_out = w_br_a[0].astype(BF16), w_br_b[0].astype(BF16), w_o[0].astype(BF16)
    w_gu, w_d = w_gate_up[0].astype(BF16), w_down[0].astype(BF16)
    gq = jnp.tile(b_q_g[0], B_HEADS)[None, :]
    gk = jnp.tile(b_k_g[0], B_KV_HEADS)[None, :]
    blk = jnp.arange(MXU_DIM) // HEAD_DIM
    bd = jnp.where(blk[:, None] == blk[None, :], 1.0 / HEAD_DIM, 0.0).astype(BF16)
    lam_vecs = jnp.concatenate([lambda_q1, lambda_k1, lambda_q2, lambda_k2], axis=0)
    gcol = a_subln_g[0][:, None]
    wr_t = jnp.zeros((LANES, d), F32).at[:N_GROUPS].set(w_coarse[0].T).at[N_GROUPS:N_GROUPS + N_EXPERTS].set(w_fine[0].T)
    br_col = jnp.zeros((LANES, 1), F32).at[:N_GROUPS, 0].set(b_coarse[0]).at[N_GROUPS:N_GROUPS + N_EXPERTS, 0].set(b_fine[0])
    ln1g, ln1b, ln2g, ln2b = ln1_g, ln1_b, ln2_g, ln2_b

    cond = jnp.concatenate([c_ctx[None, :], c], axis=0)
    mod = _modulation(cond, w_ada[0], b_ada).reshape(1 + n_lat, 6, d)
    mod_ctx, mod_lat = mod[:1], mod[1:]

    cak = cache_a_k[:, 0].reshape(n_lat * past, 1024).astype(BF16)
    cavt = jnp.swapaxes(cache_a_v[:, 0].reshape(n_lat, past, 1024), 1, 2).astype(BF16)
    cbk = jnp.repeat(cache_b_k[:, 0], 2, axis=2).reshape(n_lat * past, 512).astype(BF16)
    cbvt = jnp.swapaxes(cache_b_v[:, 0].reshape(n_lat, past, 256), 1, 2).astype(BF16)

    diff_params = (lam_vecs, gcol)

    tri = (jnp.arange(TOKEN_TILE)[:, None] <= jnp.arange(TOKEN_TILE)[None, :]).astype(BF16)

    def tail(x2d, oa, ob, mod_x, tag):
        x1, h2x, cls, rank, counts = _post(x2d, oa, ob, mod_x, w_gate, w_a, w_b, w_out, ln1g, ln1b, wr_t, br_col,
                                           tri, "post_" + tag)
        rows, blk_elo, blk_ehi, blk_valid, nb_used = _route_plan(cls, rank, counts, x2d.shape[0])
        y_rows = _moe(h2x, rows, rows, blk_elo, blk_ehi, blk_valid, nb_used, w_gu, w_d, "moe_" + tag)
        return _final(x1, y_rows, mod_x, ln2g, ln2b, "final_" + tag)

    xp = x_prompt.reshape(n_ctx * s_ctx, d)
    qa, ka, vat, qb, kbd, vbt, nak, nav, nbk, nbv = _inproj(xp, mod_ctx, w_qkv, gq, gk, bd, None, n_ctx, s_ctx, True)
    oa = _attention(qa, ka, vat, None, diff_params, n_ctx, s_ctx, s_ctx, "attn_a_ctx")
    ob = _attention(qb, kbd, vbt, None, None, n_ctx, s_ctx, s_ctx, "attn_b_ctx")
    y_prompt = tail(xp, oa, ob, mod_ctx, "ctx").reshape(n_ctx, s_ctx, d)

    xs = x_sample.reshape(n_lat * s_lat, d)
    qa, ka, vat, qb, kbd, vbt = _inproj(xs, mod_lat, w_qkv, gq, gk, bd, _rope_tables(s_lat), n_lat, s_lat, False)
    oa = _attention(qa, ka, vat, (cak, cavt), diff_params, n_lat, s_lat, s_lat, "attn_a_lat")
    ob = _attention(qb, kbd, vbt, (cbk, cbvt), None, n_lat, s_lat, s_lat, "attn_b_lat")
    y_sample = tail(xs, oa, ob, mod_lat, "lat").reshape(n_lat, s_lat, d)

    return (y_prompt, y_sample,
            nak.reshape(n_ctx, 1, s_ctx, A_HEADS, 2, HEAD_DIM),
            nav.reshape(n_ctx, 1, s_ctx, A_HEADS, 2 * HEAD_DIM),
            nbk.reshape(n_ctx, 1, s_ctx, B_KV_HEADS, HEAD_DIM),
            nbv.reshape(n_ctx, 1, s_ctx, B_KV_HEADS, HEAD_DIM))
```
